```python
import jax, jax.numpy as jnp
from jax import lax
import numpy as np

D_MODEL = 2048
BATCH = 4
SEQ = 8192
DEPTH = 1

MIX_WIDTH = D_MODEL
ATTN_WIDTH = MIX_WIDTH // 2
REC_WIDTH = MIX_WIDTH - ATTN_WIDTH
N_ATTN_HEADS = 8
V_HEAD_DIM = ATTN_WIDTH // N_ATTN_HEADS
QK_NOPE_DIM = 128
QK_ROPE_DIM = 64
QK_HEAD_DIM = QK_NOPE_DIM + QK_ROPE_DIM
Q_LORA_RANK = 512
KV_LORA_RANK = 256
ROPE_THETA = 10000.0
Q_BLOCK = 128
N_REC_HEADS = 8
REC_HEAD_DIM = REC_WIDTH // N_REC_HEADS
CONV_WIDTH = 4
LRU_C = 8.0
N_PEER_HEADS = 8
PEER_TOPK = 16
N_KEYS = 128
N_EXPERTS = N_KEYS * N_KEYS
PEER_QUERY_DIM = 256
PEER_HALF = PEER_QUERY_DIM // 2
TOKEN_CHUNK = 128
EPS = 1e-6
OFF_CQ = 0
OFF_CKV = OFF_CQ + Q_LORA_RANK
OFF_KR = OFF_CKV + KV_LORA_RANK
OFF_XR = OFF_KR + QK_ROPE_DIM
OFF_YG = OFF_XR + REC_WIDTH
IN_COLS = OFF_YG + REC_WIDTH

kernel_name = "hymba_mla_rglru_peer_layer"


def rms_norm(t, g):
    tf = t.astype(jnp.float32)
    y = tf * lax.rsqrt(jnp.mean(tf * tf, axis=-1, keepdims=True) + EPS)
    return (y * g.astype(jnp.float32)).astype(t.dtype)


def rope_tables(positions):
    half = QK_ROPE_DIM // 2
    inv_freq = ROPE_THETA ** (-jnp.arange(half, dtype=jnp.float32) / half)
    ang = positions.astype(jnp.float32)[..., None] * inv_freq
    return jnp.cos(ang), jnp.sin(ang)


def apply_rope(t, cos, sin):
    tf = t.astype(jnp.float32)
    t1, t2 = jnp.split(tf, 2, axis=-1)
    out = jnp.concatenate([t1 * cos - t2 * sin, t1 * sin + t2 * cos], axis=-1)
    return out.astype(t.dtype)


def causal_block_attention(q, k, v):
    S = q.shape[2]
    scale = QK_HEAD_DIM ** -0.5
    diag = jnp.tril(jnp.ones((Q_BLOCK, Q_BLOCK), dtype=bool))
    outs = []
    for blk in range(S // Q_BLOCK):
        start = blk * Q_BLOCK
        end = start + Q_BLOCK
        s = jnp.einsum('bhqd,bhkd->bhqk', q[:, :, start:end], k[:, :, :end],
                       preferred_element_type=jnp.float32) * scale
        mask = jnp.concatenate([jnp.ones((Q_BLOCK, start), dtype=bool), diag], axis=1)
        p = jax.nn.softmax(jnp.where(mask, s, -jnp.inf), axis=-1).astype(v.dtype)
        outs.append(jnp.einsum('bhqk,bhkd->bhqd', p, v[:, :, :end]))
    return jnp.concatenate(outs, axis=2)


def mla_group(c_q, c_kv, k_rope, cos, sin, q_norm_g, w_uq, kv_norm_g, w_ukv, q_head_g, k_head_g):
    B, S, _ = c_q.shape
    q = (rms_norm(c_q, q_norm_g) @ w_uq).reshape(B, S, N_ATTN_HEADS, QK_HEAD_DIM)
    kv = (rms_norm(c_kv, kv_norm_g) @ w_ukv).reshape(B, S, N_ATTN_HEADS, QK_NOPE_DIM + V_HEAD_DIM)
    k_nope, v = kv[..., :QK_NOPE_DIM], kv[..., QK_NOPE_DIM:]
    k_r = jnp.broadcast_to(k_rope[:, :, None, :], (B, S, N_ATTN_HEADS, QK_ROPE_DIM))
    k = jnp.concatenate([k_nope, k_r], axis=-1)
    q = rms_norm(q, q_head_g)
    k = rms_norm(k, k_head_g)
    c, s = cos[:, :, None, :], sin[:, :, None, :]
    q = jnp.concatenate([q[..., :QK_NOPE_DIM], apply_rope(q[..., QK_NOPE_DIM:], c, s)], axis=-1)
    k = jnp.concatenate([k[..., :QK_NOPE_DIM], apply_rope(k[..., QK_NOPE_DIM:], c, s)], axis=-1)
    o = causal_block_attention(q.transpose(0, 2, 1, 3), k.transpose(0, 2, 1, 3), v.transpose(0, 2, 1, 3))
    return o.transpose(0, 2, 1, 3).reshape(B, S, ATTN_WIDTH)


def _lin_combine(e1, e2):
    a1, b1 = e1
    a2, b2 = e2
    return a1 * a2, a2 * b1 + b2


def rglru_group(x_rec, y_gate, conv_w, conv_b, w_rg, b_rg, w_ig, b_ig, lam):
    B, S, W = x_rec.shape
    xp = jnp.pad(x_rec, ((0, 0), (CONV_WIDTH - 1, 0), (0, 0)))
    xc = conv_b + sum(xp[:, j:j + S] * conv_w[j] for j in range(CONV_WIDTH))
    xh = xc.reshape(B, S, N_REC_HEADS, REC_HEAD_DIM)
    r = jax.nn.sigmoid((jnp.einsum('bshi,hij->bshj', xh, w_rg).reshape(B, S, W) + b_rg).astype(jnp.float32))
    i = jax.nn.sigmoid((jnp.einsum('bshi,hij->bshj', xh, w_ig).reshape(B, S, W) + b_ig).astype(jnp.float32))
    log_a = -LRU_C * r * jax.nn.softplus(-lam.astype(jnp.float32))
    a = jnp.exp(log_a)
    mult = jnp.sqrt(-jnp.expm1(2.0 * log_a))
    b = mult * i * xc.astype(jnp.float32)
    _, h = lax.associative_scan(_lin_combine, (a, b), axis=1)
    out = jax.nn.gelu(y_gate.astype(jnp.float32), approximate=False) * h
    return out.astype(x_rec.dtype)


def peer_ffn(xn, w_q, keys1, keys2, u_tab, v_tab):
    B, S, D = xn.shape
    xt = xn.reshape((B * S) // TOKEN_CHUNK, TOKEN_CHUNK, D)

    def chunk(xc):
        q = (xc @ w_q).reshape(TOKEN_CHUNK, N_PEER_HEADS, 2, PEER_HALF).astype(jnp.float32)
        s1 = jnp.einsum('chd,hnd->chn', q[:, :, 0], keys1.astype(jnp.float32))
        s2 = jnp.einsum('chd,hnd->chn', q[:, :, 1], keys2.astype(jnp.float32))
        v1, i1 = lax.top_k(s1, PEER_TOPK)
        v2, i2 = lax.top_k(s2, PEER_TOPK)
        cand = (v1[..., :, None] + v2[..., None, :]).reshape(TOKEN_CHUNK, N_PEER_HEADS, PEER_TOPK * PEER_TOPK)
        top, pos = lax.top_k(cand, PEER_TOPK)
        e = (jnp.take_along_axis(i1, pos // PEER_TOPK, axis=-1) * N_KEYS
             + jnp.take_along_axis(i2, pos % PEER_TOPK, axis=-1))
        g = jax.nn.softmax(top, axis=-1)
        ug = jnp.take(u_tab, e, axis=0)
        act = jax.nn.gelu(jnp.einsum('chkd,cd->chk', ug, xc).astype(jnp.float32), approximate=False)
        vg = jnp.take(v_tab, e, axis=0)
        return jnp.einsum('chk,chkd->cd', (g * act).astype(xc.dtype), vg)

    return lax.map(chunk, xt).reshape(B, S, D)


def setup_inputs(seed: int = 0) -> dict:
    key = jax.random.key(seed)
    ks = jax.random.split(key, 32)
    L, D = DEPTH, D_MODEL
    f32 = jnp.float32

    def nrm(k, shape, scale):
        return jax.random.normal(k, shape, f32) * scale

    def gain(k, n):
        return 1.0 + 0.05 * jax.random.normal(k, (L, n), f32)

    x = jax.random.normal(ks[0], (BATCH, SEQ, D), f32)
    start = jax.random.randint(ks[1], (BATCH, 1), 0, 4096, dtype=jnp.int32)
    positions = start + jnp.arange(SEQ, dtype=jnp.int32)[None, :]
    ac = jax.random.uniform(ks[2], (L, REC_WIDTH), f32, 0.9, 0.999)
    a0 = ac ** (1.0 / LRU_C)
    lru_lambda = jnp.log(a0) - jnp.log1p(-a0)
    return {
        "x": x,
        "positions": positions,
        "mix_norm_g": gain(ks[3], D),
        "w_in": nrm(ks[4], (L, D, IN_COLS), D ** -0.5),
        "q_norm_g": gain(ks[5], Q_LORA_RANK),
        "w_uq": nrm(ks[6], (L, Q_LORA_RANK, N_ATTN_HEADS * QK_HEAD_DIM), Q_LORA_RANK ** -0.5),
        "kv_norm_g": gain(ks[7], KV_LORA_RANK),
        "w_ukv": nrm(ks[8], (L, KV_LORA_RANK, N_ATTN_HEADS * (QK_NOPE_DIM + V_HEAD_DIM)), KV_LORA_RANK ** -0.5),
        "q_head_norm_g": gain(ks[9], QK_HEAD_DIM),
        "k_head_norm_g": gain(ks[10], QK_HEAD_DIM),
        "conv_w": nrm(ks[11], (L, CONV_WIDTH, REC_WIDTH), CONV_WIDTH ** -0.5),
        "conv_b": nrm(ks[12], (L, REC_WIDTH), 0.01),
        "w_rgate": nrm(ks[13], (L, N_REC_HEADS, REC_HEAD_DIM, REC_HEAD_DIM), REC_HEAD_DIM ** -0.5),
        "b_rgate": nrm(ks[14], (L, REC_WIDTH), 0.1),
        "w_igate": nrm(ks[15], (L, N_REC_HEADS, REC_HEAD_DIM, REC_HEAD_DIM), REC_HEAD_DIM ** -0.5),
        "b_igate": nrm(ks[16], (L, REC_WIDTH), 0.1),
        "lru_lambda": lru_lambda,
        "attn_out_norm_g": gain(ks[17], ATTN_WIDTH),
        "rec_out_norm_g": gain(ks[18], REC_WIDTH),
        "w_out": nrm(ks[19], (L, MIX_WIDTH, D), MIX_WIDTH ** -0.5),
        "ffn_norm_g": gain(ks[20], D),
        "peer_w_q": nrm(ks[21], (L, D, N_PEER_HEADS * PEER_QUERY_DIM), D ** -0.5),
        "peer_keys_1": nrm(ks[22], (L, N_PEER_HEADS, N_KEYS, PEER_HALF), PEER_HALF ** -0.5),
        "peer_keys_2": nrm(ks[23], (L, N_PEER_HEADS, N_KEYS, PEER_HALF), PEER_HALF ** -0.5),
        "peer_u": nrm(ks[24], (L, N_EXPERTS, D), D ** -0.5),
        "peer_v": nrm(ks[25], (L, N_EXPERTS, D), (N_PEER_HEADS * PEER_TOPK) ** -0.5),
    }


def reference(x, positions, mix_norm_g, w_in, q_norm_g, w_uq, kv_norm_g, w_ukv, q_head_norm_g,
              k_head_norm_g, conv_w, conv_b, w_rgate, b_rgate, w_igate, b_igate, lru_lambda,
              attn_out_norm_g, rec_out_norm_g, w_out, ffn_norm_g, peer_w_q, peer_keys_1,
              peer_keys_2, peer_u, peer_v):
    cos, sin = rope_tables(positions)
    for l in range(DEPTH):
        h = rms_norm(x, mix_norm_g[l])
        proj = h @ w_in[l]
        c_q = proj[..., OFF_CQ:OFF_CKV]
        c_kv = proj[..., OFF_CKV:OFF_KR]
        k_rope = proj[..., OFF_KR:OFF_XR]
        x_rec = proj[..., OFF_XR:OFF_YG]
        y_gate = proj[..., OFF_YG:IN_COLS]
        attn = mla_group(c_q, c_kv, k_rope, cos, sin, q_norm_g[l], w_uq[l], kv_norm_g[l], w_ukv[l],
                         q_head_norm_g[l], k_head_norm_g[l])
        rec = rglru_group(x_rec, y_gate, conv_w[l], conv_b[l], w_rgate[l], b_rgate[l], w_igate[l],
                          b_igate[l], lru_lambda[l])
        mixed = jnp.concatenate([rms_norm(attn, attn_out_norm_g[l]), rms_norm(rec, rec_out_norm_g[l])], axis=-1)
        x = x + mixed @ w_out[l]
        x = x + peer_ffn(rms_norm(x, ffn_norm_g[l]), peer_w_q[l], peer_keys_1[l], peer_keys_2[l],
                         peer_u[l], peer_v[l])
    return x
```

```python
import functools
import math

import jax
import jax.numpy as jnp
from jax import lax
from jax.experimental import pallas as pl
from jax.experimental.pallas import tpu as pltpu

EPS = 1e-6
LANES = 128
N_HEADS = 8
NOPE = 128
ROPE = 64
QK_DIM = NOPE + ROPE
QK_PAD = 256
V_DIM = 128
Q_RANK = 512
KV_RANK = 256
REC_W = 1024
ATTN_W = 1024
CONV_W = 4
LRU_C = 8.0
ROPE_THETA = 10000.0
TOPK = 16
N_KEYS = 128
PEER_HALF = 128
N_SLOTS = N_HEADS * TOPK
NEG = -1e30
VMEM_LIMIT = 56 * 1024 * 1024

_f32 = jnp.float32
_bf16 = jnp.bfloat16


def _cparams(sem):
    return pltpu.CompilerParams(dimension_semantics=sem, vmem_limit_bytes=VMEM_LIMIT)


def _rms(t, g):
    ms = jnp.mean(t * t, axis=-1, keepdims=True)
    return t * lax.rsqrt(ms + EPS) * g


def _gelu(t):
    return 0.5 * t * (1.0 + lax.erf(t * (1.0 / math.sqrt(2.0))))


def _proj_kernel(x_ref, g_ref, w_ref, cq_ref, ckv_ref, kr_ref, xr_ref, yg_ref):
    h = _rms(x_ref[...], g_ref[...]).astype(_bf16)

    def mm(lo, hi):
        return jnp.dot(h, w_ref[:, lo:hi], preferred_element_type=_f32)

    cq_ref[...] = mm(0, 512)
    ckv_ref[...] = mm(512, 768)
    kr_ref[...] = mm(768, 896)
    xr_ref[...] = mm(896, 1920)
    yg_ref[...] = mm(1920, 2944)


def _proj(x2, g, w_in_p, tile):
    n, d = x2.shape
    cols = w_in_p.shape[1]
    widths = (Q_RANK, KV_RANK, LANES, REC_W, REC_W)
    return pl.pallas_call(
        _proj_kernel,
        grid=(n // tile,),
        in_specs=[
            pl.BlockSpec((tile, d), lambda i: (i, 0)),
            pl.BlockSpec((1, d), lambda i: (0, 0)),
            pl.BlockSpec((d, cols), lambda i: (0, 0)),
        ],
        out_specs=[pl.BlockSpec((tile, w), lambda i: (i, 0)) for w in widths],
        out_shape=[jax.ShapeDtypeStruct((n, w), _f32) for w in widths],
        compiler_params=_cparams(("arbitrary",)),
        name="proj",
    )(x2, g, w_in_p)


def _mla_prep_kernel(cq_ref, ckv_ref, kr_ref, pos_ref, invf_ref, qg_ref, wuq_ref, kvg_ref, wukv_ref,
                     qhg_ref, khg_ref, q_ref, k_ref, v_ref):
    tile = cq_ref.shape[0]
    ang = pos_ref[...].astype(_f32) * invf_ref[...]
    cosv = jnp.cos(ang)
    sinv = jnp.sin(ang)
    lane = lax.broadcasted_iota(jnp.int32, (tile, LANES), 1)
    sin_signed = jnp.where(lane < ROPE // 2, -sinv, jnp.where(lane < ROPE, sinv, 0.0))

    def rope(t):
        swapped = jnp.where(lane < ROPE // 2, pltpu.roll(t, LANES - ROPE // 2, 1), pltpu.roll(t, ROPE // 2, 1))
        return t * cosv + swapped * sin_signed

    scale = QK_DIM ** -0.5
    qf = jnp.dot(_rms(cq_ref[...], qg_ref[...]).astype(_bf16), wuq_ref[...], preferred_element_type=_f32)
    kvf = jnp.dot(_rms(ckv_ref[...], kvg_ref[...]).astype(_bf16), wukv_ref[...], preferred_element_type=_f32)
    qhg = qhg_ref[...]
    khg = khg_ref[...]
    kr = kr_ref[...]
    kr_ss = jnp.sum(kr * kr, axis=-1, keepdims=True)
    kr_base = rope(kr * khg[:, NOPE:])
    for h in range(N_HEADS):
        qh = qf[:, h * QK_PAD:(h + 1) * QK_PAD]
        r = lax.rsqrt(jnp.sum(qh * qh, axis=-1, keepdims=True) * (1.0 / QK_DIM) + EPS) * scale
        qn = qh * r * qhg
        q_ref[0, h] = jnp.concatenate([qn[:, :NOPE], rope(qn[:, NOPE:])], axis=-1).astype(_bf16)
        kn = kvf[:, h * QK_PAD:h * QK_PAD + NOPE]
        rk = lax.rsqrt((jnp.sum(kn * kn, axis=-1, keepdims=True) + kr_ss) * (1.0 / QK_DIM) + EPS)
        k_ref[0, h] = jnp.concatenate([kn * rk * khg[:, :NOPE], kr_base * rk], axis=-1).astype(_bf16)
        v_ref[0, h] = kvf[:, h * QK_PAD + NOPE:(h + 1) * QK_PAD].astype(_bf16)


def _mla_prep(cq, ckv, kr, pos, invf, qg, wuq_p, kvg, wukv, qhg_p, khg_p, b, s, tile):
    nt = s // tile
    tok = lambda w: pl.BlockSpec((tile, w), lambda bi, ti: (bi * nt + ti, 0))
    full = lambda a: pl.BlockSpec(a.shape, lambda bi, ti: (0,) * a.ndim)
    head_out = lambda w: pl.BlockSpec((1, N_HEADS, tile, w), lambda bi, ti: (bi, 0, ti, 0))
    return pl.pallas_call(
        _mla_prep_kernel,
        grid=(b, nt),
        in_specs=[tok(Q_RANK), tok(KV_RANK), tok(LANES), tok(1), full(invf), full(qg), full(wuq_p), full(kvg),
                  full(wukv), full(qhg_p), full(khg_p)],
        out_specs=[head_out(QK_PAD), head_out(QK_PAD), head_out(V_DIM)],
        out_shape=[jax.ShapeDtypeStruct((b, N_HEADS, s, QK_PAD), _bf16),
                   jax.ShapeDtypeStruct((b, N_HEADS, s, QK_PAD), _bf16),
                   jax.ShapeDtypeStruct((b, N_HEADS, s, V_DIM), _bf16)],
        compiler_params=_cparams(("arbitrary", "arbitrary")),
        name="mla_prep",
    )(cq, ckv, kr, pos, invf, qg, wuq_p, kvg, wukv, qhg_p, khg_p)


def _attn_kernel(q_ref, k_ref, v_ref, o_ref, m_ref, l_ref, acc_ref):
    tq = q_ref.shape[2]
    qi = pl.program_id(2)
    q = q_ref[0, 0]
    m_ref[...] = jnp.full(m_ref.shape, NEG, _f32)
    l_ref[...] = jnp.zeros(l_ref.shape, _f32)
    acc_ref[...] = jnp.zeros(acc_ref.shape, _f32)

    def step(j, masked):
        start = pl.multiple_of(j * tq, tq)
        kj = k_ref[0, 0, pl.ds(start, tq), :]
        vj = v_ref[0, 0, pl.ds(start, tq), :]
        s = lax.dot_general(q, kj, (((1,), (1,)), ((), ())), preferred_element_type=_f32)
        if masked:
            row = lax.broadcasted_iota(jnp.int32, s.shape, 0)
            col = lax.broadcasted_iota(jnp.int32, s.shape, 1)
            s = jnp.where(col <= row, s, NEG)
        m_prev = m_ref[...]
        m_new = jnp.maximum(m_prev, jnp.max(s, axis=-1, keepdims=True))
        alpha = jnp.exp(m_prev - m_new)
        p = jnp.exp(s - m_new[:, :1])
        l_ref[...] = alpha * l_ref[...] + jnp.sum(p, axis=-1, keepdims=True)
        acc_ref[...] = alpha * acc_ref[...] + jnp.dot(p.astype(_bf16), vj, preferred_element_type=_f32)
        m_ref[...] = m_new

    def body(j, c):
        step(j, False)
        return c

    lax.fori_loop(0, qi, body, 0)
    step(qi, True)
    o_ref[0] = acc_ref[...] / l_ref[...]


def _attention(q, k, v, tq):
    b, h, s, _ = q.shape
    return pl.pallas_call(
        _attn_kernel,
        grid=(b, h, s // tq),
        in_specs=[
            pl.BlockSpec((1, 1, tq, QK_PAD), lambda bi, hi, qi: (bi, hi, qi, 0)),
            pl.BlockSpec((1, 1, s, QK_PAD), lambda bi, hi, qi: (bi, hi, 0, 0)),
            pl.BlockSpec((1, 1, s, V_DIM), lambda bi, hi, qi: (bi, hi, 0, 0)),
        ],
        out_specs=pl.BlockSpec((1, tq, V_DIM), lambda bi, hi, qi: (bi, qi, hi)),
        out_shape=jax.ShapeDtypeStruct((b, s, h * V_DIM), _f32),
        scratch_shapes=[pltpu.VMEM((tq, LANES), _f32), pltpu.VMEM((tq, LANES), _f32),
                        pltpu.VMEM((tq, V_DIM), _f32)],
        compiler_params=_cparams(("arbitrary", "arbitrary", "arbitrary")),
        name="attn",
    )(q, k, v)


def _rglru_kernel(xr_ref, yg_ref, cw_ref, cb_ref, wr_ref, br_ref, wi_ref, bi_ref, lam_ref, o_ref,
                  ext_ref, h_ref):
    tile = xr_ref.shape[0]
    ti = pl.program_id(1)

    @pl.when(ti == 0)
    def _():
        ext_ref[0:8, :] = jnp.zeros((8, REC_W), _f32)
        h_ref[...] = jnp.zeros(h_ref.shape, _f32)

    x = xr_ref[...]
    ext_ref[8:, :] = x
    xc = cb_ref[...] + cw_ref[CONV_W - 1:CONV_W, :] * x
    for d in range(1, CONV_W):
        xc = xc + cw_ref[CONV_W - 1 - d:CONV_W - d, :] * ext_ref[8 - d:8 - d + tile, :]
    ext_ref[0:8, :] = x[tile - 8:, :]

    xcb = xc.astype(_bf16)
    rs, is_ = [], []
    for h in range(N_HEADS):
        xh = xcb[:, h * LANES:(h + 1) * LANES]
        rs.append(jnp.dot(xh, wr_ref[h], preferred_element_type=_f32))
        is_.append(jnp.dot(xh, wi_ref[h], preferred_element_type=_f32))
    r = jax.nn.sigmoid(jnp.concatenate(rs, axis=-1) + br_ref[...])
    i = jax.nn.sigmoid(jnp.concatenate(is_, axis=-1) + bi_ref[...])
    nl = -lam_ref[...]
    softplus = jnp.maximum(nl, 0.0) + jnp.log1p(jnp.exp(-jnp.abs(nl)))
    a = jnp.exp(-LRU_C * r * softplus)
    bb = jnp.sqrt(1.0 - a * a) * i * xc

    row = lax.broadcasted_iota(jnp.int32, (tile, REC_W), 0)
    d = 1
    while d < tile:
        keep = row >= d
        a_sh = jnp.where(keep, pltpu.roll(a, d, 0), 1.0)
        b_sh = jnp.where(keep, pltpu.roll(bb, d, 0), 0.0)
        bb = a * b_sh + bb
        a = a * a_sh
        d *= 2
    hh = bb + a * h_ref[0:1, :]
    h_ref[...] = jnp.broadcast_to(hh[tile - 1:tile, :], h_ref.shape)
    o_ref[...] = _gelu(yg_ref[...]) * hh


def _rglru(xr, yg, cw, cb, wr, br, wi, bi, lam, b, s, tile):
    nt = s // tile
    tok = pl.BlockSpec((tile, REC_W), lambda bi_, ti: (bi_ * nt + ti, 0))
    full = lambda a: pl.BlockSpec(a.shape, lambda bi_, ti: (0,) * a.ndim)
    return pl.pallas_call(
        _rglru_kernel,
        grid=(b, nt),
        in_specs=[tok, tok, full(cw), full(cb), full(wr), full(br), full(wi), full(bi), full(lam)],
        out_specs=tok,
        out_shape=jax.ShapeDtypeStruct((b * s, REC_W), _f32),
        scratch_shapes=[pltpu.VMEM((tile + 8, REC_W), _f32), pltpu.VMEM((8, REC_W), _f32)],
        compiler_params=_cparams(("arbitrary", "arbitrary")),
        name="rglru",
    )(xr, yg, cw, cb, wr, br, wi, bi, lam)


def _out_proj_kernel(x_ref, at_ref, rc_ref, ag_ref, rg_ref, wo_ref, fg_ref, x1_ref, xn_ref):
    an = _rms(at_ref[...], ag_ref[...]).astype(_bf16)
    rn = _rms(rc_ref[...], rg_ref[...]).astype(_bf16)
    y = jnp.dot(an, wo_ref[0:ATTN_W, :], preferred_element_type=_f32)
    y = y + jnp.dot(rn, wo_ref[ATTN_W:, :], preferred_element_type=_f32)
    x1 = x_ref[...] + y
    x1_ref[...] = x1
    xn_ref[...] = _rms(x1, fg_ref[...]).astype(_bf16)


def _out_proj(x2, attn, rec, ag, rg, wo, fg, tile):
    n, d = x2.shape
    tok = lambda w: pl.BlockSpec((tile, w), lambda i: (i, 0))
    full = lambda a: pl.BlockSpec(a.shape, lambda i: (0,) * a.ndim)
    return pl.pallas_call(
        _out_proj_kernel,
        grid=(n // tile,),
        in_specs=[tok(d), tok(ATTN_W), tok(REC_W), full(ag), full(rg), full(wo), full(fg)],
        out_specs=[tok(d), tok(d)],
        out_shape=[jax.ShapeDtypeStruct((n, d), _f32), jax.ShapeDtypeStruct((n, d), _bf16)],
        compiler_params=_cparams(("arbitrary",)),
        name="out_proj",
    )(x2, attn, rec, ag, rg, wo, fg)


def _extract_top(s, ids, big):
    vals, sel = [], []
    for _ in range(TOPK):
        m = jnp.max(s, axis=0, keepdims=True)
        i = jnp.min(jnp.where(s == m, ids, big), axis=0, keepdims=True)
        vals.append(m)
        sel.append(i)
        s = jnp.where(ids == i, NEG, s)
    return vals, sel


def _peer_topk_kernel(xn_ref, wq_ref, k1_ref, k2_ref, e_ref, g_ref, v1_ref, i1_ref, v2_ref, i2_ref,
                      tv_ref, et_ref, gt_ref):
    tile = xn_ref.shape[0]
    xn = xn_ref[...]
    key_id = lax.broadcasted_iota(jnp.int32, (N_KEYS, tile), 0)
    n_rows = TOPK + 8 * (TOPK - 1)
    crow = lax.broadcasted_iota(jnp.int32, (n_rows, tile), 0)
    cq = jnp.where(crow < TOPK, 0, ((crow - TOPK) >> 3) + 1)
    cp = jnp.where(crow < TOPK, crow, (crow - TOPK) & 7)
    cand_id = cp * TOPK + cq
    cand_ok = (cp + 1) * (cq + 1) <= TOPK

    def head(h, carry):
        qh = jnp.dot(xn, wq_ref[h], preferred_element_type=_f32).astype(_bf16)
        for half, (kref, vref, iref) in enumerate(((k1_ref, v1_ref, i1_ref), (k2_ref, v2_ref, i2_ref))):
            st = lax.dot_general(kref[h], qh[:, half * PEER_HALF:(half + 1) * PEER_HALF],
                                 (((1,), (1,)), ((), ())), preferred_element_type=_f32)
            vals, sel = _extract_top(st, key_id, N_KEYS)
            for kk in range(TOPK):
                vref[kk:kk + 1, :] = vals[kk]
                iref[kk:kk + 1, :] = sel[kk]
        v1 = v1_ref[...]
        i1 = i1_ref[...] * N_KEYS
        cv = [v1 + v2_ref[0:1, :]]
        ce = [i1 + i2_ref[0:1, :]]
        for qq in range(1, TOPK):
            cv.append(v1[0:8] + v2_ref[qq:qq + 1, :])
            ce.append(i1[0:8] + i2_ref[qq:qq + 1, :])
        cv = jnp.where(cand_ok, jnp.concatenate(cv, axis=0), NEG)
        ce = jnp.concatenate(ce, axis=0)
        top, sel = _extract_top(cv, cand_id, TOPK * TOPK)
        for kk in range(TOPK):
            tv_ref[kk:kk + 1, :] = top[kk]
            et_ref[pl.ds(h * TOPK + kk, 1), :] = jnp.max(jnp.where(cand_id == sel[kk], ce, -1), axis=0,
                                                        keepdims=True)
        tv = tv_ref[...]
        ex = jnp.exp(tv - tv[0:1, :])
        gt_ref[pl.ds(pl.multiple_of(h * TOPK, TOPK), TOPK), :] = ex / jnp.sum(ex, axis=0, keepdims=True)
        return carry

    lax.fori_loop(0, N_HEADS, head, 0)
    e_ref[...] = et_ref[...].T
    g_ref[...] = gt_ref[...].T


def _peer_topk(xn, wq_h, k1, k2, tile):
    n, d = xn.shape
    full = lambda a: pl.BlockSpec(a.shape, lambda i: (0,) * a.ndim)
    slot = pl.BlockSpec((tile, N_SLOTS), lambda i: (i, 0))
    return pl.pallas_call(
        _peer_topk_kernel,
        grid=(n // tile,),
        in_specs=[pl.BlockSpec((tile, d), lambda i: (i, 0)), full(wq_h), full(k1), full(k2)],
        out_specs=[slot, slot],
        out_shape=[jax.ShapeDtypeStruct((n, N_SLOTS), jnp.int32), jax.ShapeDtypeStruct((n, N_SLOTS), _f32)],
        scratch_shapes=[pltpu.VMEM((TOPK, tile), _f32), pltpu.VMEM((TOPK, tile), jnp.int32),
                        pltpu.VMEM((TOPK, tile), _f32), pltpu.VMEM((TOPK, tile), jnp.int32),
                        pltpu.VMEM((TOPK, tile), _f32),
                        pltpu.VMEM((N_SLOTS, tile), jnp.int32), pltpu.VMEM((N_SLOTS, tile), _f32)],
        compiler_params=_cparams(("arbitrary",)),
        name="peer_topk",
    )(xn, wq_h, k1, k2)


def _peer_w_kernel(e_ref, g_ref, w_ref, scr_ref):
    tile = e_ref.shape[0]
    sub = lax.broadcasted_iota(jnp.int32, (N_KEYS, N_SLOTS), 0)

    def token(n, c):
        e = e_ref[pl.ds(n, 1), :]
        g = g_ref[pl.ds(n, 1), :]
        ga = jnp.where(sub == (e >> 7), g, 0.0).astype(_bf16)
        hb = jnp.where(sub == (e & (N_KEYS - 1)), 1.0, 0.0).astype(_bf16)
        scr_ref[pl.ds(pl.multiple_of(n * N_KEYS, N_KEYS), N_KEYS), :] = lax.dot_general(
            ga, hb, (((1,), (1,)), ((), ())), preferred_element_type=_f32)
        return c

    lax.fori_loop(0, tile, token, 0)

    def regroup(r, c):
        w_ref[r] = scr_ref[pl.ds(r, tile, stride=N_KEYS), :].astype(_bf16)
        return c

    lax.fori_loop(0, N_KEYS, regroup, 0)


def _peer_w(e, g, tile):
    n = e.shape[0]
    slot = pl.BlockSpec((tile, N_SLOTS), lambda i: (i, 0))
    return pl.pallas_call(
        _peer_w_kernel,
        grid=(n // tile,),
        in_specs=[slot, slot],
        out_specs=pl.BlockSpec((N_KEYS, tile, N_KEYS), lambda i: (0, i, 0)),
        out_shape=jax.ShapeDtypeStruct((N_KEYS, n, N_KEYS), _bf16),
        scratch_shapes=[pltpu.VMEM((tile * N_KEYS, N_KEYS), _f32)],
        compiler_params=_cparams(("arbitrary",)),
        name="peer_w",
    )(e, g)


def _peer_ffn_kernel(xn_ref, x1_ref, ut_ref, v_ref, w_ref, o_ref):
    @pl.when(pl.program_id(1) == 0)
    def _():
        o_ref[...] = x1_ref[...]

    a = jnp.dot(xn_ref[...], ut_ref[...], preferred_element_type=_f32)
    w = jnp.concatenate([w_ref[0], w_ref[1]], axis=-1).astype(_f32)
    p = (_gelu(a) * w).astype(_bf16)
    o_ref[...] += jnp.dot(p, v_ref[...], preferred_element_type=_f32)


def _peer_ffn(xn, x1, ut, v, wb, tile, eblk):
    n, d = xn.shape
    ne = v.shape[0]
    rows = eblk // N_KEYS
    return pl.pallas_call(
        _peer_ffn_kernel,
        grid=(n // tile, ne // eblk),
        in_specs=[
            pl.BlockSpec((tile, d), lambda i, j: (i, 0)),
            pl.BlockSpec((tile, d), lambda i, j: (i, 0)),
            pl.BlockSpec((d, eblk), lambda i, j: (0, j)),
            pl.BlockSpec((eblk, d), lambda i, j: (j, 0)),
            pl.BlockSpec((rows, tile, N_KEYS), lambda i, j: (j, i, 0)),
        ],
        out_specs=pl.BlockSpec((tile, d), lambda i, j: (i, 0)),
        out_shape=jax.ShapeDtypeStruct((n, d), _f32),
        compiler_params=_cparams(("arbitrary", "arbitrary")),
        name="peer_ffn",
    )(xn, x1, ut, v, wb)


def _pad_rope_cols(w):
    lead = w.shape[:-1]
    w = w.reshape(*lead, N_HEADS, QK_DIM)
    w = jnp.pad(w, [(0, 0)] * len(lead) + [(0, 0), (0, QK_PAD - QK_DIM)])
    return w.reshape(*lead, N_HEADS * QK_PAD)


def _layer(x2, pos, invf, b, s, tiles, mix_g, w_in, q_g, w_uq, kv_g, w_ukv, qh_g, kh_g, conv_w, conv_b, w_rg,
           b_rg, w_ig, b_ig, lam, ao_g, ro_g, w_out, ffn_g, w_q, k1, k2, u_tab, v_tab):
    d = x2.shape[1]
    row = lambda a: a.reshape(1, -1)
    off_kr = Q_RANK + KV_RANK
    off_xr = off_kr + ROPE
    w_in_p = jnp.concatenate(
        [w_in[:, :off_xr], jnp.zeros((d, LANES - ROPE), w_in.dtype), w_in[:, off_xr:]], axis=1).astype(_bf16)
    cq, ckv, kr, xr, yg = _proj(x2, row(mix_g), w_in_p, tiles["proj"])

    pad_g = lambda g: jnp.pad(g, (0, QK_PAD - QK_DIM)).reshape(1, QK_PAD)
    q, k, v = _mla_prep(cq, ckv, kr, pos, invf, row(q_g), _pad_rope_cols(w_uq).astype(_bf16), row(kv_g),
                        w_ukv.astype(_bf16), pad_g(qh_g), pad_g(kh_g), b, s, tiles["prep"])
    attn = _attention(q, k, v, tiles["attn"]).reshape(b * s, ATTN_W)

    rec = _rglru(xr, yg, conv_w, row(conv_b), w_rg.astype(_bf16), row(b_rg), w_ig.astype(_bf16), row(b_ig),
                 row(lam), b, s, tiles["rec"])

    x1, xn = _out_proj(x2, attn, rec, row(ao_g), row(ro_g), w_out.astype(_bf16), row(ffn_g), tiles["out"])

    wq_h = w_q.reshape(d, N_HEADS, 2 * PEER_HALF).transpose(1, 0, 2).astype(_bf16)
    e, g = _peer_topk(xn, wq_h, k1.astype(_bf16), k2.astype(_bf16), tiles["topk"])
    wb = _peer_w(e, g, tiles["w"])
    return _peer_ffn(xn, x1, u_tab.T.astype(_bf16), v_tab.astype(_bf16), wb, tiles["ffn"], tiles["eblk"])


_TILES = dict(proj=256, prep=256, attn=512, rec=256, out=256, topk=256, w=128, ffn=512, eblk=256)


def _forward(tiles, x, positions, mix_norm_g, w_in, q_norm_g, w_uq, kv_norm_g, w_ukv, q_head_norm_g,
             k_head_norm_g, conv_w, conv_b, w_rgate, b_rgate, w_igate, b_igate, lru_lambda, attn_out_norm_g,
             rec_out_norm_g, w_out, ffn_norm_g, peer_w_q, peer_keys_1, peer_keys_2, peer_u, peer_v):
    b, s, d = x.shape
    half = ROPE // 2
    freq = ROPE_THETA ** (-jnp.arange(half, dtype=_f32) / half)
    invf = jnp.concatenate([freq, freq, jnp.zeros((LANES - ROPE,), _f32)]).reshape(1, LANES)
    pos = positions.reshape(b * s, 1)
    x2 = x.reshape(b * s, d)
    for l in range(mix_norm_g.shape[0]):
        x2 = _layer(x2, pos, invf, b, s, tiles, mix_norm_g[l], w_in[l], q_norm_g[l], w_uq[l], kv_norm_g[l],
                    w_ukv[l], q_head_norm_g[l], k_head_norm_g[l], conv_w[l], conv_b[l], w_rgate[l], b_rgate[l],
                    w_igate[l], b_igate[l], lru_lambda[l], attn_out_norm_g[l], rec_out_norm_g[l], w_out[l],
                    ffn_norm_g[l], peer_w_q[l], peer_keys_1[l], peer_keys_2[l], peer_u[l], peer_v[l])
    return x2.reshape(b, s, d)


def kernel(x, positions, mix_norm_g, w_in, q_norm_g, w_uq, kv_norm_g, w_ukv, q_head_norm_g, k_head_norm_g, conv_w, conv_b, w_rgate, b_rgate, w_igate, b_igate, lru_lambda, attn_out_norm_g, rec_out_norm_g, w_out, ffn_norm_g, peer_w_q, peer_keys_1, peer_keys_2, peer_u, peer_v):
    return _forward(_TILES, x, positions, mix_norm_g, w_in, q_norm_g, w_uq, kv_norm_g, w_ukv, q_head_norm_g,
                    k_head_norm_g, conv_w, conv_b, w_rgate, b_rgate, w_igate, b_igate, lru_lambda,
                    attn_out_norm_g, rec_out_norm_g, w_out, ffn_norm_g, peer_w_q, peer_keys_1, peer_keys_2,
                    peer_u, peer_v)
```

```python
import functools
import math

import jax
import jax.numpy as jnp
from jax import lax
from jax.experimental import pallas as pl
from jax.experimental.pallas import tpu as pltpu

EPS = 1e-6
LANES = 128
N_HEADS = 8
NOPE = 128
ROPE = 64
QK_DIM = NOPE + ROPE
QK_PAD = 256
V_DIM = 128
Q_RANK = 512
KV_RANK = 256
REC_W = 1024
ATTN_W = 1024
CONV_W = 4
LRU_C = 8.0
ROPE_THETA = 10000.0
TOPK = 16
N_KEYS = 128
PEER_HALF = 128
N_SLOTS = N_HEADS * TOPK
TOPK_COLS = 256
NEG = -1e30
VMEM_LIMIT = 56 * 1024 * 1024

_f32 = jnp.float32
_bf16 = jnp.bfloat16


def _cparams(sem):
    return pltpu.CompilerParams(dimension_semantics=sem, vmem_limit_bytes=VMEM_LIMIT)


def _rms(t, g):
    ms = jnp.mean(t * t, axis=-1, keepdims=True)
    return t * lax.rsqrt(ms + EPS) * g


def _gelu(t):
    return 0.5 * t * (1.0 + lax.erf(t * (1.0 / math.sqrt(2.0))))


def _proj_kernel(x_ref, g_ref, w_ref, cq_ref, ckv_ref, kr_ref, xr_ref, yg_ref):
    h = _rms(x_ref[...], g_ref[...]).astype(_bf16)

    def mm(lo, hi):
        return jnp.dot(h, w_ref[:, lo:hi], preferred_element_type=_f32)

    cq_ref[...] = mm(0, 512)
    ckv_ref[...] = mm(512, 768)
    kr_ref[...] = mm(768, 896)
    xr_ref[...] = mm(896, 1920)
    yg_ref[...] = mm(1920, 2944)


def _proj(x2, g, w_in_p, tile):
    n, d = x2.shape
    cols = w_in_p.shape[1]
    widths = (Q_RANK, KV_RANK, LANES, REC_W, REC_W)
    return pl.pallas_call(
        _proj_kernel,
        grid=(n // tile,),
        in_specs=[
            pl.BlockSpec((tile, d), lambda i: (i, 0)),
            pl.BlockSpec((1, d), lambda i: (0, 0)),
            pl.BlockSpec((d, cols), lambda i: (0, 0)),
        ],
        out_specs=[pl.BlockSpec((tile, w), lambda i: (i, 0)) for w in widths],
        out_shape=[jax.ShapeDtypeStruct((n, w), _f32) for w in widths],
        compiler_params=_cparams(("arbitrary",)),
        name="proj",
    )(x2, g, w_in_p)


def _mla_prep_kernel(cq_ref, ckv_ref, kr_ref, pos_ref, invf_ref, qg_ref, wuq_ref, kvg_ref, wukv_ref,
                     qhg_ref, khg_ref, q_ref, k_ref, v_ref):
    tile = cq_ref.shape[0]
    ang = pos_ref[...].astype(_f32) * invf_ref[...]
    cosv = jnp.cos(ang)
    sinv = jnp.sin(ang)
    lane = lax.broadcasted_iota(jnp.int32, (tile, LANES), 1)
    sin_signed = jnp.where(lane < ROPE // 2, -sinv, jnp.where(lane < ROPE, sinv, 0.0))

    def rope(t):
        swapped = jnp.where(lane < ROPE // 2, pltpu.roll(t, LANES - ROPE // 2, 1), pltpu.roll(t, ROPE // 2, 1))
        return t * cosv + swapped * sin_signed

    scale = QK_DIM ** -0.5 * math.log2(math.e)
    qf =jnp.dot(_rms(cq_ref[...], qg_ref[...]).astype(_bf16), wuq_ref[...], preferred_element_type=_f32)
    kvf = jnp.dot(_rms(ckv_ref[...], kvg_ref[...]).astype(_bf16), wukv_ref[...], preferred_element_type=_f32)
    qhg = qhg_ref[...]
    khg = khg_ref[...]
    kr = kr_ref[...]
    kr_ss = jnp.sum(kr * kr, axis=-1, keepdims=True)
    kr_base = rope(kr * khg[:, NOPE:])
    for h in range(N_HEADS):
        qh = qf[:, h * QK_PAD:(h + 1) * QK_PAD]
        r = lax.rsqrt(jnp.sum(qh * qh, axis=-1, keepdims=True) * (1.0 / QK_DIM) + EPS) * scale
        qn = qh * r * qhg
        q_ref[0, h] = jnp.concatenate([qn[:, :NOPE], rope(qn[:, NOPE:])], axis=-1).astype(_bf16)
        kn = kvf[:, h * QK_PAD:h * QK_PAD + NOPE]
        rk = lax.rsqrt((jnp.sum(kn * kn, axis=-1, keepdims=True) + kr_ss) * (1.0 / QK_DIM) + EPS)
        k_ref[0, h] = jnp.concatenate([kn * rk * khg[:, :NOPE], kr_base * rk], axis=-1).astype(_bf16)
        v_ref[0, h] = kvf[:, h * QK_PAD + NOPE:(h + 1) * QK_PAD].astype(_bf16)


def _mla_prep(cq, ckv, kr, pos, invf, qg, wuq_p, kvg, wukv, qhg_p, khg_p, b, s, tile):
    nt = s // tile
    tok = lambda w: pl.BlockSpec((tile, w), lambda bi, ti: (bi * nt + ti, 0))
    full = lambda a: pl.BlockSpec(a.shape, lambda bi, ti: (0,) * a.ndim)
    head_out = lambda w: pl.BlockSpec((1, N_HEADS, tile, w), lambda bi, ti: (bi, 0, ti, 0))
    return pl.pallas_call(
        _mla_prep_kernel,
        grid=(b, nt),
        in_specs=[tok(Q_RANK), tok(KV_RANK), tok(LANES), tok(1), full(invf), full(qg), full(wuq_p), full(kvg),
                  full(wukv), full(qhg_p), full(khg_p)],
        out_specs=[head_out(QK_PAD), head_out(QK_PAD), head_out(V_DIM)],
        out_shape=[jax.ShapeDtypeStruct((b, N_HEADS, s, QK_PAD), _bf16),
                   jax.ShapeDtypeStruct((b, N_HEADS, s, QK_PAD), _bf16),
                   jax.ShapeDtypeStruct((b, N_HEADS, s, V_DIM), _bf16)],
        compiler_params=_cparams(("arbitrary", "arbitrary")),
        name="mla_prep",
    )(cq, ckv, kr, pos, invf, qg, wuq_p, kvg, wukv, qhg_p, khg_p)


def _attn_kernel(q_ref, k_ref, v_ref, o_ref, m_ref, l_ref, acc_ref):
    tc = q_ref.shape[2] // 2
    qi = pl.program_id(2)
    m_ref[...] = jnp.full(m_ref.shape, NEG, _f32)
    l_ref[...] = jnp.zeros(l_ref.shape, _f32)
    acc_ref[...] = jnp.zeros(acc_ref.shape, _f32)

    def step(half, j, masked):
        start = pl.multiple_of(j * tc, tc)
        q = q_ref[0, 0, half * tc:(half + 1) * tc, :]
        kj = k_ref[0, 0, pl.ds(start, tc), :]
        vj = v_ref[0, 0, pl.ds(start, tc), :]
        s = lax.dot_general(q, kj, (((1,), (1,)), ((), ())), preferred_element_type=_f32)
        if masked:
            row = lax.broadcasted_iota(jnp.int32, s.shape, 0)
            col = lax.broadcasted_iota(jnp.int32, s.shape, 1)
            s = jnp.where(col <= row, s, NEG)
        m_prev = m_ref[half]
        m_new = jnp.maximum(m_prev, jnp.max(s, axis=-1, keepdims=True))
        alpha = jnp.exp2(m_prev - m_new)
        p = jnp.exp2(s - m_new[:, :1])
        l_ref[half] = alpha * l_ref[half] + jnp.sum(p, axis=-1, keepdims=True)
        acc_ref[half] = alpha * acc_ref[half] + jnp.dot(p.astype(_bf16), vj, preferred_element_type=_f32)
        m_ref[half] = m_new

    def body(j, c):
        step(0, j, False)
        step(1, j, False)
        return c

    lax.fori_loop(0, 2 * qi, body, 0)
    step(0, 2 * qi, True)
    step(1, 2 * qi, False)
    step(1, 2 * qi + 1, True)
    for half in range(2):
        o_ref[0, half * tc:(half + 1) * tc, :] = acc_ref[half] / l_ref[half]


def _attention(q, k, v, tq):
    b, h, s, _ = q.shape
    return pl.pallas_call(
        _attn_kernel,
        grid=(b, h, s // tq),
        in_specs=[
            pl.BlockSpec((1, 1, tq, QK_PAD), lambda bi, hi, qi: (bi, hi, qi, 0)),
            pl.BlockSpec((1, 1, s, QK_PAD), lambda bi, hi, qi: (bi, hi, 0, 0)),
            pl.BlockSpec((1, 1, s, V_DIM), lambda bi, hi, qi: (bi, hi, 0, 0)),
        ],
        out_specs=pl.BlockSpec((1, tq, V_DIM), lambda bi, hi, qi: (bi, qi, hi)),
        out_shape=jax.ShapeDtypeStruct((b, s, h * V_DIM), _f32),
        scratch_shapes=[pltpu.VMEM((2, tq // 2, LANES), _f32), pltpu.VMEM((2, tq // 2, LANES), _f32),
                        pltpu.VMEM((2, tq // 2, V_DIM), _f32)],
        compiler_params=_cparams(("arbitrary", "arbitrary", "arbitrary")),
        name="attn",
    )(q, k, v)


def _rglru_kernel(xr_ref, yg_ref, cw_ref, cb_ref, wr_ref, br_ref, wi_ref, bi_ref, lam_ref, o_ref,
                  ext_ref, h_ref):
    tile = xr_ref.shape[0]
    ti = pl.program_id(1)

    @pl.when(ti == 0)
    def _():
        ext_ref[0:8, :] = jnp.zeros((8, REC_W), _f32)
        h_ref[...] = jnp.zeros(h_ref.shape, _f32)

    x = xr_ref[...]
    ext_ref[8:, :] = x
    xc = cb_ref[...] + cw_ref[CONV_W - 1:CONV_W, :] * x
    for d in range(1, CONV_W):
        xc = xc + cw_ref[CONV_W - 1 - d:CONV_W - d, :] * ext_ref[8 - d:8 - d + tile, :]
    ext_ref[0:8, :] = x[tile - 8:, :]

    xcb = xc.astype(_bf16)
    rs, is_ = [], []
    for h in range(N_HEADS):
        xh = xcb[:, h * LANES:(h + 1) * LANES]
        rs.append(jnp.dot(xh, wr_ref[h], preferred_element_type=_f32))
        is_.append(jnp.dot(xh, wi_ref[h], preferred_element_type=_f32))
    r = jax.nn.sigmoid(jnp.concatenate(rs, axis=-1) + br_ref[...])
    i = jax.nn.sigmoid(jnp.concatenate(is_, axis=-1) + bi_ref[...])
    nl = -lam_ref[...]
    softplus = jnp.maximum(nl, 0.0) + jnp.log1p(jnp.exp(-jnp.abs(nl)))
    a = jnp.exp(-LRU_C * r * softplus)
    bb = jnp.sqrt(1.0 - a * a) * i * xc

    row = lax.broadcasted_iota(jnp.int32, (tile, REC_W), 0)
    d = 1
    while d < tile:
        keep = row >= d
        a_sh = jnp.where(keep, pltpu.roll(a, d, 0), 1.0)
        b_sh = jnp.where(keep, pltpu.roll(bb, d, 0), 0.0)
        bb = a * b_sh + bb
        a = a * a_sh
        d *= 2
    hh = bb + a * h_ref[0:1, :]
    h_ref[...] = jnp.broadcast_to(hh[tile - 1:tile, :], h_ref.shape)
    o_ref[...] = _gelu(yg_ref[...]) * hh


def _rglru(xr, yg, cw, cb, wr, br, wi, bi, lam, b, s, tile):
    nt = s // tile
    tok = pl.BlockSpec((tile, REC_W), lambda bi_, ti: (bi_ * nt + ti, 0))
    full = lambda a: pl.BlockSpec(a.shape, lambda bi_, ti: (0,) * a.ndim)
    return pl.pallas_call(
        _rglru_kernel,
        grid=(b, nt),
        in_specs=[tok, tok, full(cw), full(cb), full(wr), full(br), full(wi), full(bi), full(lam)],
        out_specs=tok,
        out_shape=jax.ShapeDtypeStruct((b * s, REC_W), _f32),
        scratch_shapes=[pltpu.VMEM((tile + 8, REC_W), _f32), pltpu.VMEM((8, REC_W), _f32)],
        compiler_params=_cparams(("arbitrary", "arbitrary")),
        name="rglru",
    )(xr, yg, cw, cb, wr, br, wi, bi, lam)


def _out_proj_kernel(x_ref, at_ref, rc_ref, ag_ref, rg_ref, wo_ref, fg_ref, x1_ref, xn_ref):
    an = _rms(at_ref[...], ag_ref[...]).astype(_bf16)
    rn = _rms(rc_ref[...], rg_ref[...]).astype(_bf16)
    y = jnp.dot(an, wo_ref[0:ATTN_W, :], preferred_element_type=_f32)
    y = y + jnp.dot(rn, wo_ref[ATTN_W:, :], preferred_element_type=_f32)
    x1 = x_ref[...] + y
    x1_ref[...] = x1
    xn_ref[...] = _rms(x1, fg_ref[...]).astype(_bf16)


def _out_proj(x2, attn, rec, ag, rg, wo, fg, tile):
    n, d = x2.shape
    tok = lambda w: pl.BlockSpec((tile, w), lambda i: (i, 0))
    full = lambda a: pl.BlockSpec(a.shape, lambda i: (0,) * a.ndim)
    return pl.pallas_call(
        _out_proj_kernel,
        grid=(n // tile,),
        in_specs=[tok(d), tok(ATTN_W), tok(REC_W), full(ag), full(rg), full(wo), full(fg)],
        out_specs=[tok(d), tok(d)],
        out_shape=[jax.ShapeDtypeStruct((n, d), _f32), jax.ShapeDtypeStruct((n, d), _bf16)],
        compiler_params=_cparams(("arbitrary",)),
        name="out_proj",
    )(x2, attn, rec, ag, rg, wo, fg)


def _extract_top(s, ids, big, payload=None):
    vals, sel, pay = [], [], []
    for _ in range(TOPK):
        m = jnp.max(s, axis=0, keepdims=True)
        i = jnp.min(jnp.where(s == m, ids, big), axis=0, keepdims=True)
        hit = ids == i
        vals.append(m)
        sel.append(i)
        if payload is not None:
            pay.append(jnp.max(jnp.where(hit, payload, -1), axis=0, keepdims=True))
        s = jnp.where(hit, NEG, s)
    return vals, sel, pay


def _peer_topk_kernel(xn_ref, wq_ref, k1_ref, k2_ref, e_ref, g_ref, v1_ref, i1_ref, v2_ref, i2_ref,
                      tv_ref, et_ref, gt_ref):
    tile = xn_ref.shape[0]
    xn = xn_ref[...]
    key_id = lax.broadcasted_iota(jnp.int32, (N_KEYS, TOPK_COLS), 0)
    n_rows = TOPK + 8 * (TOPK - 1)
    crow = lax.broadcasted_iota(jnp.int32, (n_rows, TOPK_COLS), 0)
    cq = jnp.where(crow < TOPK, 0, ((crow - TOPK) >> 3) + 1)
    cp = jnp.where(crow < TOPK, crow, (crow - TOPK) & 7)
    cand_id = cp * TOPK + cq
    cand_ok = (cp + 1) * (cq + 1) <= TOPK

    def head(h, carry):
        qh = jnp.dot(xn, wq_ref[h], preferred_element_type=_f32).astype(_bf16)
        for lt in range(tile // TOPK_COLS):
            cols = slice(lt * TOPK_COLS, (lt + 1) * TOPK_COLS)
            for half, (kref, vref, iref) in enumerate(((k1_ref, v1_ref, i1_ref), (k2_ref, v2_ref, i2_ref))):
                st = lax.dot_general(kref[h], qh[cols, half * PEER_HALF:(half + 1) * PEER_HALF],
                                     (((1,), (1,)), ((), ())), preferred_element_type=_f32)
                vals, sel, _ = _extract_top(st, key_id, N_KEYS)
                for kk in range(TOPK):
                    vref[kk:kk + 1, cols] = vals[kk]
                    iref[kk:kk + 1, cols] = sel[kk]
            v1 = v1_ref[:, cols]
            i1 = i1_ref[:, cols] * N_KEYS
            cv = [v1 + v2_ref[0:1, cols]]
            ce = [i1 + i2_ref[0:1, cols]]
            for qq in range(1, TOPK):
                cv.append(v1[0:8] + v2_ref[qq:qq + 1, cols])
                ce.append(i1[0:8] + i2_ref[qq:qq + 1, cols])
            cv = jnp.where(cand_ok, jnp.concatenate(cv, axis=0), NEG)
            ce = jnp.concatenate(ce, axis=0)
            top, _, picked = _extract_top(cv, cand_id, TOPK * TOPK, payload=ce)
            for kk in range(TOPK):
                tv_ref[kk:kk + 1, cols] = top[kk]
                et_ref[lt, pl.ds(h * TOPK + kk, 1), :] = picked[kk]
        tv = tv_ref[...]
        ex = jnp.exp(tv - tv[0:1, :])
        gt_ref[pl.ds(pl.multiple_of(h * TOPK, TOPK), TOPK), :] = ex / jnp.sum(ex, axis=0, keepdims=True)
        return carry

    lax.fori_loop(0, N_HEADS, head, 0)
    for lt in range(tile // TOPK_COLS):
        e_ref[lt * TOPK_COLS:(lt + 1) * TOPK_COLS, :] = et_ref[lt].T
    g_ref[...] = gt_ref[...].T


def _peer_topk(xn, wq_h, k1, k2, tile):
    n, d = xn.shape
    full = lambda a: pl.BlockSpec(a.shape, lambda i: (0,) * a.ndim)
    slot = pl.BlockSpec((tile, N_SLOTS), lambda i: (i, 0))
    return pl.pallas_call(
        _peer_topk_kernel,
        grid=(n // tile,),
        in_specs=[pl.BlockSpec((tile, d), lambda i: (i, 0)), full(wq_h), full(k1), full(k2)],
        out_specs=[slot, slot],
        out_shape=[jax.ShapeDtypeStruct((n, N_SLOTS), jnp.int32), jax.ShapeDtypeStruct((n, N_SLOTS), _f32)],
        scratch_shapes=[pltpu.VMEM((TOPK, tile), _f32), pltpu.VMEM((TOPK, tile), jnp.int32),
                        pltpu.VMEM((TOPK, tile), _f32), pltpu.VMEM((TOPK, tile), jnp.int32),
                        pltpu.VMEM((TOPK, tile), _f32),
                        pltpu.VMEM((tile // TOPK_COLS, N_SLOTS, TOPK_COLS), jnp.int32),
                        pltpu.VMEM((N_SLOTS, tile), _f32)],
        compiler_params=_cparams(("arbitrary",)),
        name="peer_topk",
    )(xn, wq_h, k1, k2)


W_PITCH = N_KEYS + 8
W_GROUP = 8


def _peer_w_kernel(e_ref, g_ref, w_ref, scr_ref):
    tile = e_ref.shape[0]
    sub = lax.broadcasted_iota(jnp.int32, (N_KEYS, N_SLOTS), 0)

    def group(i, c):
        base = pl.multiple_of(i * W_GROUP, W_GROUP)
        e8 = e_ref[pl.ds(base, W_GROUP), :]
        g8 = g_ref[pl.ds(base, W_GROUP), :]
        for t in range(W_GROUP):
            e = e8[t:t + 1, :]
            ga = jnp.where(sub == (e >> 7), g8[t:t + 1, :], 0.0).astype(_bf16)
            hb = jnp.where(sub == (e & (N_KEYS - 1)), 1.0, 0.0).astype(_bf16)
            scr_ref[pl.ds(pl.multiple_of((base + t) * W_PITCH, 8), N_KEYS), :] = lax.dot_general(
                ga, hb, (((1,), (1,)), ((), ())), preferred_element_type=_f32)
        return c

    lax.fori_loop(0, tile // W_GROUP, group, 0)

    def regroup(i, c):
        for t in range(4):
            r = i * 4 + t
            w_ref[r] = scr_ref[pl.ds(r, tile, stride=W_PITCH), :].astype(_bf16)
        return c

    lax.fori_loop(0, N_KEYS // 4, regroup, 0)


def _peer_w(e, g, tile):
    n = e.shape[0]
    slot = pl.BlockSpec((tile, N_SLOTS), lambda i: (i, 0))
    return pl.pallas_call(
        _peer_w_kernel,
        grid=(n // tile,),
        in_specs=[slot, slot],
        out_specs=pl.BlockSpec((N_KEYS, tile, N_KEYS), lambda i: (0, i, 0)),
        out_shape=jax.ShapeDtypeStruct((N_KEYS, n, N_KEYS), _bf16),
        scratch_shapes=[pltpu.VMEM((tile * W_PITCH, N_KEYS), _f32)],
        compiler_params=_cparams(("arbitrary",)),
        name="peer_w",
    )(e, g)


def _peer_ffn_kernel(xn_ref, x1_ref, ut_ref, v_ref, w_ref, o_ref, pa_ref, pb_ref):
    j = pl.program_id(1)
    rows = w_ref.shape[0]

    @pl.when(j == 0)
    def _():
        o_ref[...] = x1_ref[...]
        pb_ref[...] = jnp.zeros(pb_ref.shape, _bf16)

    def step(p_new_ref, p_prev_ref):
        a = jnp.dot(xn_ref[...], ut_ref[...], preferred_element_type=_f32)
        w = jnp.concatenate([w_ref[r] for r in range(rows)], axis=-1).astype(_f32)
        p_new_ref[...] = (_gelu(a) * w).astype(_bf16)
        o_ref[...] += jnp.dot(p_prev_ref[...], v_ref[...], preferred_element_type=_f32)

    @pl.when(lax.rem(j, 2) == 0)
    def _():
        step(pa_ref, pb_ref)

    @pl.when(lax.rem(j, 2) == 1)
    def _():
        step(pb_ref, pa_ref)


def _peer_ffn(xn, x1, ut, v, wb, tile, eblk):
    n, d = xn.shape
    nblk = v.shape[0] // eblk
    rows = eblk // N_KEYS
    cur = lambda j: jnp.minimum(j, nblk - 1)
    prev = lambda j: jnp.maximum(j - 1, 0)
    return pl.pallas_call(
        _peer_ffn_kernel,
        grid=(n // tile, nblk + 1),
        in_specs=[
            pl.BlockSpec((tile, d), lambda i, j: (i, 0)),
            pl.BlockSpec((tile, d), lambda i, j: (i, 0)),
            pl.BlockSpec((d, eblk), lambda i, j: (0, cur(j))),
            pl.BlockSpec((eblk, d), lambda i, j: (prev(j), 0)),
            pl.BlockSpec((rows, tile, N_KEYS), lambda i, j: (cur(j), i, 0)),
        ],
        out_specs=pl.BlockSpec((tile, d), lambda i, j: (i, 0)),
        out_shape=jax.ShapeDtypeStruct((n, d), _f32),
        scratch_shapes=[pltpu.VMEM((tile, eblk), _bf16), pltpu.VMEM((tile, eblk), _bf16)],
        compiler_params=_cparams(("arbitrary", "arbitrary")),
        name="peer_ffn",
    )(xn, x1, ut, v, wb)


def _pad_rope_cols(w):
    lead = w.shape[:-1]
    w = w.reshape(*lead, N_HEADS, QK_DIM)
    w = jnp.pad(w, [(0, 0)] * len(lead) + [(0, 0), (0, QK_PAD - QK_DIM)])
    return w.reshape(*lead, N_HEADS * QK_PAD)


def _layer(x2, pos, invf, b, s, tiles, mix_g, w_in, q_g, w_uq, kv_g, w_ukv, qh_g, kh_g, conv_w, conv_b, w_rg,
           b_rg, w_ig, b_ig, lam, ao_g, ro_g, w_out, ffn_g, w_q, k1, k2, u_tab, v_tab):
    d = x2.shape[1]
    row = lambda a: a.reshape(1, -1)
    off_kr = Q_RANK + KV_RANK
    off_xr = off_kr + ROPE
    w_in_p = jnp.concatenate(
        [w_in[:, :off_xr], jnp.zeros((d, LANES - ROPE), w_in.dtype), w_in[:, off_xr:]], axis=1).astype(_bf16)
    cq, ckv, kr, xr, yg = _proj(x2, row(mix_g), w_in_p, tiles["proj"])

    pad_g = lambda g: jnp.pad(g, (0, QK_PAD - QK_DIM)).reshape(1, QK_PAD)
    q, k, v = _mla_prep(cq, ckv, kr, pos, invf, row(q_g), _pad_rope_cols(w_uq).astype(_bf16), row(kv_g),
                        w_ukv.astype(_bf16), pad_g(qh_g), pad_g(kh_g), b, s, tiles["prep"])
    attn = _attention(q, k, v, tiles["attn"]).reshape(b * s, ATTN_W)

    rec = _rglru(xr, yg, conv_w, row(conv_b), w_rg.astype(_bf16), row(b_rg), w_ig.astype(_bf16), row(b_ig),
                 row(lam), b, s, tiles["rec"])

    x1, xn = _out_proj(x2, attn, rec, row(ao_g), row(ro_g), w_out.astype(_bf16), row(ffn_g), tiles["out"])

    wq_h = w_q.reshape(d, N_HEADS, 2 * PEER_HALF).transpose(1, 0, 2).astype(_bf16)
    e, g = _peer_topk(xn, wq_h, k1.astype(_bf16), k2.astype(_bf16), tiles["topk"])
    wb = _peer_w(e, g, tiles["w"])
    return _peer_ffn(xn, x1, u_tab.T.astype(_bf16), v_tab.astype(_bf16), wb, tiles["ffn"], tiles["eblk"])


_TILES = dict(proj=256, prep=256, attn=1024, rec=256, out=256, topk=256, w=128, ffn=512, eblk=512)


def _forward(tiles, x, positions, mix_norm_g, w_in, q_norm_g, w_uq, kv_norm_g, w_ukv, q_head_norm_g,
             k_head_norm_g, conv_w, conv_b, w_rgate, b_rgate, w_igate, b_igate, lru_lambda, attn_out_norm_g,
             rec_out_norm_g, w_out, ffn_norm_g, peer_w_q, peer_keys_1, peer_keys_2, peer_u, peer_v):
    b, s, d = x.shape
    half = ROPE // 2
    freq = ROPE_THETA ** (-jnp.arange(half, dtype=_f32) / half)
    invf = jnp.concatenate([freq, freq, jnp.zeros((LANES - ROPE,), _f32)]).reshape(1, LANES)
    pos = positions.reshape(b * s, 1)
    x2 = x.reshape(b * s, d)
    for l in range(mix_norm_g.shape[0]):
        x2 = _layer(x2, pos, invf, b, s, tiles, mix_norm_g[l], w_in[l], q_norm_g[l], w_uq[l], kv_norm_g[l],
                    w_ukv[l], q_head_norm_g[l], k_head_norm_g[l], conv_w[l], conv_b[l], w_rgate[l], b_rgate[l],
                    w_igate[l], b_igate[l], lru_lambda[l], attn_out_norm_g[l], rec_out_norm_g[l], w_out[l],
                    ffn_norm_g[l], peer_w_q[l], peer_keys_1[l], peer_keys_2[l], peer_u[l], peer_v[l])
    return x2.reshape(b, s, d)


def kernel(x, positions, mix_norm_g, w_in, q_norm_g, w_uq, kv_norm_g, w_ukv, q_head_norm_g, k_head_norm_g, conv_w, conv_b, w_rgate, b_rgate, w_igate, b_igate, lru_lambda, attn_out_norm_g, rec_out_norm_g, w_out, ffn_norm_g, peer_w_q, peer_keys_1, peer_keys_2, peer_u, peer_v):
    return _forward(_TILES, x, positions, mix_norm_g, w_in, q_norm_g, w_uq, kv_norm_g, w_ukv, q_head_norm_g,
                    k_head_norm_g, conv_w, conv_b, w_rgate, b_rgate, w_igate, b_igate, lru_lambda,
                    attn_out_norm_g, rec_out_norm_g, w_out, ffn_norm_g, peer_w_q, peer_keys_1, peer_keys_2,
                    peer_u, peer_v)
```

```python
import functools
import math

import jax
import jax.numpy as jnp
from jax import lax
from jax.experimental import pallas as pl
from jax.experimental.pallas import tpu as pltpu

EPS = 1e-6
LANES = 128
N_HEADS = 8
NOPE = 128
ROPE = 64
QK_DIM = NOPE + ROPE
QK_PAD = 256
V_DIM = 128
Q_RANK = 512
KV_RANK = 256
REC_W = 1024
ATTN_W = 1024
CONV_W = 4
LRU_C = 8.0
ROPE_THETA = 10000.0
TOPK = 16
N_KEYS = 128
PEER_HALF = 128
N_SLOTS = N_HEADS * TOPK
TOPK_COLS = 256
TOPK_CHAINS = 2
NEG = -1e30
VMEM_LIMIT = 56 * 1024 * 1024

_f32 = jnp.float32
_bf16 = jnp.bfloat16


def _cparams(sem):
    return pltpu.CompilerParams(dimension_semantics=sem, vmem_limit_bytes=VMEM_LIMIT)


def _rms(t, g):
    ms = jnp.mean(t * t, axis=-1, keepdims=True)
    return t * lax.rsqrt(ms + EPS) * g


def _gelu(t):
    return 0.5 * t * (1.0 + lax.erf(t * (1.0 / math.sqrt(2.0))))


def _proj_kernel(x_ref, g_ref, w_ref, cq_ref, ckv_ref, kr_ref, xr_ref, yg_ref):
    h = _rms(x_ref[...], g_ref[...]).astype(_bf16)

    def mm(lo, hi):
        return jnp.dot(h, w_ref[:, lo:hi], preferred_element_type=_f32)

    cq_ref[...] = mm(0, 512)
    ckv_ref[...] = mm(512, 768)
    kr_ref[...] = mm(768, 896)
    xr_ref[...] = mm(896, 1920)
    yg_ref[...] = mm(1920, 2944)


def _proj(x2, g, w_in_p, tile):
    n, d = x2.shape
    cols = w_in_p.shape[1]
    widths = (Q_RANK, KV_RANK, LANES, REC_W, REC_W)
    return pl.pallas_call(
        _proj_kernel,
        grid=(n // tile,),
        in_specs=[
            pl.BlockSpec((tile, d), lambda i: (i, 0)),
            pl.BlockSpec((1, d), lambda i: (0, 0)),
            pl.BlockSpec((d, cols), lambda i: (0, 0)),
        ],
        out_specs=[pl.BlockSpec((tile, w), lambda i: (i, 0)) for w in widths],
        out_shape=[jax.ShapeDtypeStruct((n, w), _f32) for w in widths],
        compiler_params=_cparams(("arbitrary",)),
        name="proj",
    )(x2, g, w_in_p)


def _mla_prep_kernel(cq_ref, ckv_ref, kr_ref, pos_ref, invf_ref, qg_ref, wuq_ref, kvg_ref, wukv_ref,
                     qhg_ref, khg_ref, q_ref, k_ref, v_ref):
    tile = cq_ref.shape[0]
    ang = pos_ref[...].astype(_f32) * invf_ref[...]
    cosv = jnp.cos(ang)
    sinv = jnp.sin(ang)
    lane = lax.broadcasted_iota(jnp.int32, (tile, LANES), 1)
    sin_signed = jnp.where(lane < ROPE // 2, -sinv, jnp.where(lane < ROPE, sinv, 0.0))

    def rope(t):
        swapped = jnp.where(lane < ROPE // 2, pltpu.roll(t, LANES - ROPE // 2, 1), pltpu.roll(t, ROPE // 2, 1))
        return t * cosv + swapped * sin_signed

    scale = QK_DIM ** -0.5 * math.log2(math.e)
    qf =jnp.dot(_rms(cq_ref[...], qg_ref[...]).astype(_bf16), wuq_ref[...], preferred_element_type=_f32)
    kvf = jnp.dot(_rms(ckv_ref[...], kvg_ref[...]).astype(_bf16), wukv_ref[...], preferred_element_type=_f32)
    qhg = qhg_ref[...]
    khg = khg_ref[...]
    kr = kr_ref[...]
    kr_ss = jnp.sum(kr * kr, axis=-1, keepdims=True)
    kr_base = rope(kr * khg[:, NOPE:])
    ones_col = jnp.where(lane == 0, 1.0, 0.0)
    for h in range(N_HEADS):
        qh = qf[:, h * QK_PAD:(h + 1) * QK_PAD]
        r = lax.rsqrt(jnp.sum(qh * qh, axis=-1, keepdims=True) * (1.0 / QK_DIM) + EPS) * scale
        qn = qh * r * qhg
        q_ref[0, h] = jnp.concatenate([qn[:, :NOPE], rope(qn[:, NOPE:])], axis=-1).astype(_bf16)
        kn = kvf[:, h * QK_PAD:h * QK_PAD + NOPE]
        rk = lax.rsqrt((jnp.sum(kn * kn, axis=-1, keepdims=True) + kr_ss) * (1.0 / QK_DIM) + EPS)
        k_ref[0, h] = jnp.concatenate([kn * rk * khg[:, :NOPE], kr_base * rk], axis=-1).astype(_bf16)
        v_ref[0, h] = jnp.concatenate([kvf[:, h * QK_PAD + NOPE:(h + 1) * QK_PAD], ones_col], axis=-1).astype(_bf16)


def _mla_prep(cq, ckv, kr, pos, invf, qg, wuq_p, kvg, wukv, qhg_p, khg_p, b, s, tile):
    nt = s // tile
    tok = lambda w: pl.BlockSpec((tile, w), lambda bi, ti: (bi * nt + ti, 0))
    full = lambda a: pl.BlockSpec(a.shape, lambda bi, ti: (0,) * a.ndim)
    head_out = lambda w: pl.BlockSpec((1, N_HEADS, tile, w), lambda bi, ti: (bi, 0, ti, 0))
    return pl.pallas_call(
        _mla_prep_kernel,
        grid=(b, nt),
        in_specs=[tok(Q_RANK), tok(KV_RANK), tok(LANES), tok(1), full(invf), full(qg), full(wuq_p), full(kvg),
                  full(wukv), full(qhg_p), full(khg_p)],
        out_specs=[head_out(QK_PAD), head_out(QK_PAD), head_out(V_PAD)],
        out_shape=[jax.ShapeDtypeStruct((b, N_HEADS, s, QK_PAD), _bf16),
                   jax.ShapeDtypeStruct((b, N_HEADS, s, QK_PAD), _bf16),
                   jax.ShapeDtypeStruct((b, N_HEADS, s, V_PAD), _bf16)],
        compiler_params=_cparams(("arbitrary", "arbitrary")),
        name="mla_prep",
    )(cq, ckv, kr, pos, invf, qg, wuq_p, kvg, wukv, qhg_p, khg_p)


ATT_ROWS = 256
ATT_CHUNK = 512
V_PAD = 256


def _attn_kernel(q_ref, k_ref, v_ref, o_ref, m_ref, acc_ref, sa_ref, sb_ref):
    tq = q_ref.shape[2]
    assert tq == 2 * ATT_CHUNK
    nsub = tq // ATT_ROWS
    qi = pl.program_id(2)
    n0 = 2 * qi
    m_ref[...] = jnp.full(m_ref.shape, NEG, _f32)
    acc_ref[...] = jnp.zeros(acc_ref.shape, _f32)

    def chunk(t):
        return pl.ds(pl.multiple_of(t * ATT_CHUNK, ATT_CHUNK), ATT_CHUNK)

    def scores(r, t, s_ref):
        q = q_ref[0, 0, r * ATT_ROWS:(r + 1) * ATT_ROWS, :]
        s_ref[r] = lax.dot_general(q, k_ref[0, 0, chunk(t), :], (((1,), (1,)), ((), ())),
                                   preferred_element_type=_f32)

    def update(r, t, s_ref, c=None):
        s = s_ref[r]
        if c is not None and (c + 1) * ATT_CHUNK - 1 > r * ATT_ROWS:
            row = lax.broadcasted_iota(jnp.int32, s.shape, 0) + r * ATT_ROWS
            col = lax.broadcasted_iota(jnp.int32, s.shape, 1) + c * ATT_CHUNK
            s = jnp.where(col <= row, s, NEG)
        m_prev = m_ref[r]
        m_new = jnp.maximum(m_prev, jnp.max(s, axis=-1, keepdims=True))
        alpha = jnp.exp2(m_prev - m_new)
        p = jnp.exp2(s - m_new[:, :1]).astype(_bf16)
        acc_ref[r] = jnp.concatenate([alpha, alpha], axis=-1) * acc_ref[r] + jnp.dot(
            p, v_ref[0, 0, chunk(t), :], preferred_element_type=_f32)
        m_ref[r] = m_new

    every = range(nsub)
    upper = [r for r in every if (r + 1) * ATT_ROWS > ATT_CHUNK]
    for r in every:
        scores(r, 0, sa_ref)

    def body(i, c):
        t = 2 * i
        for r in every:
            scores(r, t + 1, sb_ref)
        for r in every:
            update(r, t, sa_ref)
        for r in every:
            scores(r, t + 2, sa_ref)
        for r in every:
            update(r, t + 1, sb_ref)
        return c

    lax.fori_loop(0, qi, body, 0)
    for r in upper:
        scores(r, n0 + 1, sb_ref)
    for r in every:
        update(r, n0, sa_ref, c=0)
    for r in upper:
        update(r, n0 + 1, sb_ref, c=1)
    for r in range(nsub):
        acc = acc_ref[r]
        o_ref[0, r * ATT_ROWS:(r + 1) * ATT_ROWS, :] = acc[:, :V_DIM] / acc[:, V_DIM:V_DIM + 1]


def _attention(q, k, v, tq):
    b, h, s, _ = q.shape
    return pl.pallas_call(
        _attn_kernel,
        grid=(b, h, s // tq),
        in_specs=[
            pl.BlockSpec((1, 1, tq, QK_PAD), lambda bi, hi, qi: (bi, hi, qi, 0)),
            pl.BlockSpec((1, 1, s, QK_PAD), lambda bi, hi, qi: (bi, hi, 0, 0)),
            pl.BlockSpec((1, 1, s, V_PAD), lambda bi, hi, qi: (bi, hi, 0, 0)),
        ],
        out_specs=pl.BlockSpec((1, tq, V_DIM), lambda bi, hi, qi: (bi, qi, hi)),
        out_shape=jax.ShapeDtypeStruct((b, s, h * V_DIM), _f32),
        scratch_shapes=[pltpu.VMEM((tq // ATT_ROWS, ATT_ROWS, LANES), _f32),
                        pltpu.VMEM((tq // ATT_ROWS, ATT_ROWS, V_PAD), _f32),
                        pltpu.VMEM((tq // ATT_ROWS, ATT_ROWS, ATT_CHUNK), _f32),
                        pltpu.VMEM((tq // ATT_ROWS, ATT_ROWS, ATT_CHUNK), _f32)],
        compiler_params=_cparams(("arbitrary", "arbitrary", "arbitrary")),
        name="attn",
    )(q, k, v)


def _rglru_kernel(xr_ref, yg_ref, cw_ref, cb_ref, wr_ref, br_ref, wi_ref, bi_ref, lam_ref, o_ref,
                  ext_ref, h_ref):
    tile = xr_ref.shape[0]
    ti = pl.program_id(1)

    @pl.when(ti == 0)
    def _():
        ext_ref[0:8, :] = jnp.zeros((8, REC_W), _f32)
        h_ref[...] = jnp.zeros(h_ref.shape, _f32)

    x = xr_ref[...]
    ext_ref[8:, :] = x
    xc = cb_ref[...] + cw_ref[CONV_W - 1:CONV_W, :] * x
    for d in range(1, CONV_W):
        xc = xc + cw_ref[CONV_W - 1 - d:CONV_W - d, :] * ext_ref[8 - d:8 - d + tile, :]
    ext_ref[0:8, :] = x[tile - 8:, :]

    xcb = xc.astype(_bf16)
    rs, is_ = [], []
    for h in range(N_HEADS):
        xh = xcb[:, h * LANES:(h + 1) * LANES]
        rs.append(jnp.dot(xh, wr_ref[h], preferred_element_type=_f32))
        is_.append(jnp.dot(xh, wi_ref[h], preferred_element_type=_f32))
    r = jax.nn.sigmoid(jnp.concatenate(rs, axis=-1) + br_ref[...])
    i = jax.nn.sigmoid(jnp.concatenate(is_, axis=-1) + bi_ref[...])
    nl = -lam_ref[...]
    softplus = jnp.maximum(nl, 0.0) + jnp.log1p(jnp.exp(-jnp.abs(nl)))
    a = jnp.exp(-LRU_C * r * softplus)
    bb = jnp.sqrt(1.0 - a * a) * i * xc

    row = lax.broadcasted_iota(jnp.int32, (tile, REC_W), 0)
    d = 1
    while d < tile:
        keep = row >= d
        a_sh = jnp.where(keep, pltpu.roll(a, d, 0), 1.0)
        b_sh = jnp.where(keep, pltpu.roll(bb, d, 0), 0.0)
        bb = a * b_sh + bb
        a = a * a_sh
        d *= 2
    hh = bb + a * h_ref[0:1, :]
    h_ref[...] = jnp.broadcast_to(hh[tile - 1:tile, :], h_ref.shape)
    o_ref[...] = _gelu(yg_ref[...]) * hh


def _rglru(xr, yg, cw, cb, wr, br, wi, bi, lam, b, s, tile):
    nt = s // tile
    tok = pl.BlockSpec((tile, REC_W), lambda bi_, ti: (bi_ * nt + ti, 0))
    full = lambda a: pl.BlockSpec(a.shape, lambda bi_, ti: (0,) * a.ndim)
    return pl.pallas_call(
        _rglru_kernel,
        grid=(b, nt),
        in_specs=[tok, tok, full(cw), full(cb), full(wr), full(br), full(wi), full(bi), full(lam)],
        out_specs=tok,
        out_shape=jax.ShapeDtypeStruct((b * s, REC_W), _f32),
        scratch_shapes=[pltpu.VMEM((tile + 8, REC_W), _f32), pltpu.VMEM((8, REC_W), _f32)],
        compiler_params=_cparams(("arbitrary", "arbitrary")),
        name="rglru",
    )(xr, yg, cw, cb, wr, br, wi, bi, lam)


def _out_proj_kernel(x_ref, at_ref, rc_ref, ag_ref, rg_ref, wo_ref, fg_ref, x1_ref, xn_ref):
    an = _rms(at_ref[...], ag_ref[...]).astype(_bf16)
    rn = _rms(rc_ref[...], rg_ref[...]).astype(_bf16)
    y = jnp.dot(an, wo_ref[0:ATTN_W, :], preferred_element_type=_f32)
    y = y + jnp.dot(rn, wo_ref[ATTN_W:, :], preferred_element_type=_f32)
    x1 = x_ref[...] + y
    x1_ref[...] = x1
    xn_ref[...] = _rms(x1, fg_ref[...]).astype(_bf16)


def _out_proj(x2, attn, rec, ag, rg, wo, fg, tile):
    n, d = x2.shape
    tok = lambda w: pl.BlockSpec((tile, w), lambda i: (i, 0))
    full = lambda a: pl.BlockSpec(a.shape, lambda i: (0,) * a.ndim)
    return pl.pallas_call(
        _out_proj_kernel,
        grid=(n // tile,),
        in_specs=[tok(d), tok(ATTN_W), tok(REC_W), full(ag), full(rg), full(wo), full(fg)],
        out_specs=[tok(d), tok(d)],
        out_shape=[jax.ShapeDtypeStruct((n, d), _f32), jax.ShapeDtypeStruct((n, d), _bf16)],
        compiler_params=_cparams(("arbitrary",)),
        name="out_proj",
    )(x2, attn, rec, ag, rg, wo, fg)


def _extract_top(s, ids, big, payload=None):
    vals, sel, pay = [], [], []
    for _ in range(TOPK):
        m = jnp.max(s, axis=0, keepdims=True)
        i = jnp.min(jnp.where(s == m, ids, big), axis=0, keepdims=True)
        hit = ids == i
        vals.append(m)
        sel.append(i)
        if payload is not None:
            pay.append(jnp.max(jnp.where(hit, payload, -1), axis=0, keepdims=True))
        s = jnp.where(hit, NEG, s)
    return vals, sel, pay


CODE_ROWS = 16


def _extract_top_unique(scores, code):
    state = list(scores)
    res = [([], []) for _ in state]
    for _ in range(TOPK):
        for k, s in enumerate(state):
            m = jnp.max(s, axis=0, keepdims=True)
            hit = s == m
            res[k][0].append(m)
            res[k][1].append(jnp.dot(code, jnp.where(hit, 1.0, 0.0).astype(_bf16), preferred_element_type=_f32))
            state[k] = jnp.where(hit, NEG, s)
    return res


def _peer_topk_kernel(xn_ref, wq_ref, k1_ref, k2_ref, e_ref, g_ref, qs_ref, v1_ref, i1_ref, v2_ref, i2_ref,
                      tv_ref, pc_ref, qc_ref, et_ref, gt_ref):
    tile = xn_ref.shape[0]
    q_all = jnp.dot(xn_ref[...], wq_ref[...], preferred_element_type=_f32)
    for h in range(N_HEADS):
        qs_ref[h] = q_all[:, h * 2 * PEER_HALF:(h + 1) * 2 * PEER_HALF].astype(_bf16)
    key_id = lax.broadcasted_iota(jnp.int32, (N_KEYS, TOPK_COLS), 0)

    def cand_codes(row):
        low = row < TOPK
        mid = row < TOPK + 8 * 7
        cq = jnp.where(low, 0, jnp.where(mid, ((row - TOPK) >> 3) + 1, row - (TOPK + 8 * 7) + 8))
        cp = jnp.where(low, row, jnp.where(mid, (row - TOPK) & 7, 0))
        ok = ((cp + 1) * (cq + 1) <= TOPK) & (row < TOPK + 8 * 8)
        return cp, cq, ok

    n_rows = N_KEYS
    cp, cq, cand_ok = cand_codes(lax.broadcasted_iota(jnp.int32, (n_rows, TOPK_COLS), 0))
    cand_id = cp * TOPK + cq
    code_row = lax.broadcasted_iota(jnp.int32, (CODE_ROWS, n_rows), 0)
    code_col = lax.broadcasted_iota(jnp.int32, (CODE_ROWS, n_rows), 1)
    key_code = jnp.where(code_row == 0, code_col, jnp.where(code_row == 1, 1, 0)).astype(_f32).astype(_bf16)
    lp, lq, _ = cand_codes(code_col)
    pair_code = jnp.where(code_row == 0, lp, jnp.where(code_row == 1, lq, jnp.where(code_row == 2, 1, 0)))
    pair_code = pair_code.astype(_f32).astype(_bf16)

    def half_scores(h, qh, cols, half):
        kref = (k1_ref, k2_ref)[half]
        return lax.dot_general(kref[h], qh[cols, half * PEER_HALF:(half + 1) * PEER_HALF],
                               (((1,), (1,)), ((), ())), preferred_element_type=_f32)

    def candidates(c, cols):
        v1 = v1_ref[c, :, cols]
        i1 = i1_ref[c, :, cols] * N_KEYS
        cv = [v1 + v2_ref[c, 0:1, cols]]
        ce = [i1 + i2_ref[c, 0:1, cols]]
        for qq in range(1, 8):
            cv.append(v1[0:8] + v2_ref[c, qq:qq + 1, cols])
            ce.append(i1[0:8] + i2_ref[c, qq:qq + 1, cols])
        cv.append(v1[0:1] + v2_ref[c, 8:16, cols])
        ce.append(i1[0:1] + i2_ref[c, 8:16, cols])
        pad = n_rows - (TOPK + 8 * 8)
        cv.append(jnp.full((pad, TOPK_COLS), NEG, _f32))
        ce.append(jnp.zeros((pad, TOPK_COLS), jnp.int32))
        return jnp.where(cand_ok, jnp.concatenate(cv, axis=0), NEG), jnp.concatenate(ce, axis=0)

    def gates(c, h):
        tv = tv_ref[c]
        ex = jnp.exp(tv - tv[0:1, :])
        gt_ref[pl.ds(pl.multiple_of(h * TOPK, TOPK), TOPK), :] = ex / jnp.sum(ex, axis=0, keepdims=True)

    def head_group(i, most_hits):
        heads = [(c, i * TOPK_CHAINS + c) for c in range(TOPK_CHAINS)]
        for lt in range(tile // TOPK_COLS):
            cols = slice(lt * TOPK_COLS, (lt + 1) * TOPK_COLS)
            for c, h in heads:
                qh = qs_ref[h]
                found = _extract_top_unique([half_scores(h, qh, cols, half) for half in range(2)], key_code)
                for (vals, outs), vref, iref in zip(found, (v1_ref, v2_ref), (i1_ref, i2_ref)):
                    for kk in range(TOPK):
                        vref[c, kk:kk + 1, cols] = vals[kk]
                        iref[c, kk:kk + 1, cols] = outs[kk][0:1].astype(jnp.int32)
                        most_hits = jnp.maximum(most_hits, outs[kk][1:2])
            found = _extract_top_unique([candidates(c, cols)[0] for c, _ in heads], pair_code)
            for (c, h), (top, outs) in zip(heads, found):
                for kk in range(TOPK):
                    tv_ref[c, kk:kk + 1, cols] = top[kk]
                    pc_ref[c, kk:kk + 1, cols] = outs[kk][0:1]
                    qc_ref[c, kk:kk + 1, cols] = outs[kk][1:2]
                    most_hits = jnp.maximum(most_hits, outs[kk][2:3])
                pc = pc_ref[c, :, cols]
                qc = qc_ref[c, :, cols]
                e1 = jnp.zeros((TOPK, TOPK_COLS), jnp.int32)
                e2 = jnp.zeros((TOPK, TOPK_COLS), jnp.int32)
                for r in range(TOPK):
                    e1 = jnp.where(pc == float(r), i1_ref[c, r:r + 1, cols], e1)
                    e2 = jnp.where(qc == float(r), i2_ref[c, r:r + 1, cols], e2)
                et_ref[lt, pl.ds(pl.multiple_of(h * TOPK, TOPK), TOPK), :] = e1 * N_KEYS + e2
        for c, h in heads:
            gates(c, h)
        return most_hits

    def head_general(h, carry):
        c = 0
        qh = qs_ref[h]
        for lt in range(tile // TOPK_COLS):
            cols = slice(lt * TOPK_COLS, (lt + 1) * TOPK_COLS)
            for half, (vref, iref) in enumerate(((v1_ref, i1_ref), (v2_ref, i2_ref))):
                vals, sel, _ = _extract_top(half_scores(h, qh, cols, half), key_id, N_KEYS)
                for kk in range(TOPK):
                    vref[c, kk:kk + 1, cols] = vals[kk]
                    iref[c, kk:kk + 1, cols] = sel[kk]
            cv, ce = candidates(c, cols)
            top, _, picked = _extract_top(cv, cand_id, TOPK * TOPK, payload=ce)
            for kk in range(TOPK):
                tv_ref[c, kk:kk + 1, cols] = top[kk]
                et_ref[lt, pl.ds(h * TOPK + kk, 1), :] = picked[kk]
        gates(c, h)
        return carry

    most_hits = lax.fori_loop(0, N_HEADS // TOPK_CHAINS, head_group, jnp.zeros((1, TOPK_COLS), _f32))

    @pl.when(jnp.max(most_hits) > 1.5)
    def _():
        lax.fori_loop(0, N_HEADS, head_general, 0)

    for lt in range(tile // TOPK_COLS):
        e_ref[lt * TOPK_COLS:(lt + 1) * TOPK_COLS, :] = et_ref[lt].T
    g_ref[...] = gt_ref[...].T


def _peer_topk(xn, wq_h, k1, k2, tile):
    n, d = xn.shape
    full = lambda a: pl.BlockSpec(a.shape, lambda i: (0,) * a.ndim)
    slot = pl.BlockSpec((tile, N_SLOTS), lambda i: (i, 0))
    return pl.pallas_call(
        _peer_topk_kernel,
        grid=(n // tile,),
        in_specs=[pl.BlockSpec((tile, d), lambda i: (i, 0)), full(wq_h), full(k1), full(k2)],
        out_specs=[slot, slot],
        out_shape=[jax.ShapeDtypeStruct((n, N_SLOTS), jnp.int32), jax.ShapeDtypeStruct((n, N_SLOTS), _f32)],
        scratch_shapes=[pltpu.VMEM((N_HEADS, tile, 2 * PEER_HALF), _bf16),
                        pltpu.VMEM((TOPK_CHAINS, TOPK, tile), _f32), pltpu.VMEM((TOPK_CHAINS, TOPK, tile), jnp.int32),
                        pltpu.VMEM((TOPK_CHAINS, TOPK, tile), _f32), pltpu.VMEM((TOPK_CHAINS, TOPK, tile), jnp.int32),
                        pltpu.VMEM((TOPK_CHAINS, TOPK, tile), _f32), pltpu.VMEM((TOPK_CHAINS, TOPK, tile), _f32),
                        pltpu.VMEM((TOPK_CHAINS, TOPK, tile), _f32),
                        pltpu.VMEM((tile // TOPK_COLS, N_SLOTS, TOPK_COLS), jnp.int32),
                        pltpu.VMEM((N_SLOTS, tile), _f32)],
        compiler_params=_cparams(("arbitrary",)),
        name="peer_topk",
    )(xn, wq_h, k1, k2)


W_PITCH = N_KEYS + 8
W_GROUP = 8
W_UNROLL = 4


def _peer_w_kernel(e_ref, g_ref, w_ref, scr_ref):
    tile = e_ref.shape[0]
    sub = lax.broadcasted_iota(jnp.int32, (N_KEYS, N_SLOTS), 0)

    def group(i, c):
        for sg in range(W_UNROLL):
            tokens8(pl.multiple_of((i * W_UNROLL + sg) * W_GROUP, W_GROUP))
        return c

    def tokens8(base):
        e8 = e_ref[pl.ds(base, W_GROUP), :]
        g8 = g_ref[pl.ds(base, W_GROUP), :]
        zero = jnp.zeros((N_KEYS, N_SLOTS), _bf16)
        for t in range(0, W_GROUP, 2):
            ga, hb = [], []
            for u in (t, t + 1):
                e = e8[u:u + 1, :]
                ga.append(jnp.where(sub == (e >> 7), g8[u:u + 1, :], 0.0).astype(_bf16))
                hb.append(jnp.where(sub == (e & (N_KEYS - 1)), 1.0, 0.0).astype(_bf16))
            lhs = jnp.concatenate(ga, axis=1)
            rhs = jnp.concatenate([jnp.concatenate([hb[0], zero], axis=1),
                                   jnp.concatenate([zero, hb[1]], axis=1)], axis=0)
            w2 = lax.dot_general(lhs, rhs, (((1,), (1,)), ((), ())), preferred_element_type=_f32)
            for k, u in enumerate((t, t + 1)):
                scr_ref[pl.ds(pl.multiple_of((base + u) * W_PITCH, 8), N_KEYS), :] = w2[:, k * N_KEYS:(k + 1) * N_KEYS]

    lax.fori_loop(0, tile // (W_GROUP * W_UNROLL), group, 0)

    def regroup(i, c):
        for t in range(4):
            r = i * 4 + t
            w_ref[r] = scr_ref[pl.ds(r, tile, stride=W_PITCH), :].astype(_bf16)
        return c

    lax.fori_loop(0, N_KEYS // 4, regroup, 0)


def _peer_w(e, g, tile):
    n = e.shape[0]
    slot = pl.BlockSpec((tile, N_SLOTS), lambda i: (i, 0))
    return pl.pallas_call(
        _peer_w_kernel,
        grid=(n // tile,),
        in_specs=[slot, slot],
        out_specs=pl.BlockSpec((N_KEYS, tile, N_KEYS), lambda i: (0, i, 0)),
        out_shape=jax.ShapeDtypeStruct((N_KEYS, n, N_KEYS), _bf16),
        scratch_shapes=[pltpu.VMEM((tile * W_PITCH, N_KEYS), _f32)],
        compiler_params=_cparams(("arbitrary",)),
        name="peer_w",
    )(e, g)


def _peer_ffn_kernel(nblk, xn_ref, x1_ref, ut_ref, v_ref, w_ref, o_ref, pa_ref, pb_ref):
    j = pl.program_id(1)
    rows = w_ref.shape[0]

    @pl.when(j == 0)
    def _():
        o_ref[...] = x1_ref[...]
        pb_ref[...] = jnp.zeros(pb_ref.shape, _bf16)

    def step(p_new_ref, p_prev_ref):
        if p_new_ref is not None:
            a = jnp.dot(xn_ref[...], ut_ref[...], preferred_element_type=_f32)
            w = jnp.concatenate([w_ref[r] for r in range(rows)], axis=-1).astype(_f32)
            p_new_ref[...] = (_gelu(a) * w).astype(_bf16)
        o_ref[...] += jnp.dot(p_prev_ref[...], v_ref[...], preferred_element_type=_f32)

    last = nblk
    assert nblk % 2 == 0

    @pl.when((lax.rem(j, 2) == 0) & (j < last))
    def _():
        step(pa_ref, pb_ref)

    @pl.when(lax.rem(j, 2) == 1)
    def _():
        step(pb_ref, pa_ref)

    @pl.when(j == last)
    def _():
        step(None, pb_ref)


def _peer_ffn(xn, x1, ut, v, wb, tile, eblk):
    n, d = xn.shape
    nblk = v.shape[0] // eblk
    rows = eblk // N_KEYS
    cur = lambda j: jnp.minimum(j, nblk - 1)
    prev = lambda j: jnp.maximum(j - 1, 0)
    return pl.pallas_call(
        functools.partial(_peer_ffn_kernel, nblk),
        grid=(n // tile, nblk + 1),
        in_specs=[
            pl.BlockSpec((tile, d), lambda i, j: (i, 0)),
            pl.BlockSpec((tile, d), lambda i, j: (i, 0)),
            pl.BlockSpec((d, eblk), lambda i, j: (0, cur(j))),
            pl.BlockSpec((eblk, d), lambda i, j: (prev(j), 0)),
            pl.BlockSpec((rows, tile, N_KEYS), lambda i, j: (cur(j), i, 0)),
        ],
        out_specs=pl.BlockSpec((tile, d), lambda i, j: (i, 0)),
        out_shape=jax.ShapeDtypeStruct((n, d), _f32),
        scratch_shapes=[pltpu.VMEM((tile, eblk), _bf16), pltpu.VMEM((tile, eblk), _bf16)],
        compiler_params=_cparams(("arbitrary", "arbitrary")),
        name="peer_ffn",
    )(xn, x1, ut, v, wb)


def _pad_rope_cols(w):
    lead = w.shape[:-1]
    w = w.reshape(*lead, N_HEADS, QK_DIM)
    w = jnp.pad(w, [(0, 0)] * len(lead) + [(0, 0), (0, QK_PAD - QK_DIM)])
    return w.reshape(*lead, N_HEADS * QK_PAD)


def _layer(x2, pos, invf, b, s, tiles, mix_g, w_in, q_g, w_uq, kv_g, w_ukv, qh_g, kh_g, conv_w, conv_b, w_rg,
           b_rg, w_ig, b_ig, lam, ao_g, ro_g, w_out, ffn_g, w_q, k1, k2, u_tab, v_tab):
    d = x2.shape[1]
    row = lambda a: a.reshape(1, -1)
    off_kr = Q_RANK + KV_RANK
    off_xr = off_kr + ROPE
    w_in_p = jnp.concatenate(
        [w_in[:, :off_xr], jnp.zeros((d, LANES - ROPE), w_in.dtype), w_in[:, off_xr:]], axis=1).astype(_bf16)
    cq, ckv, kr, xr, yg = _proj(x2, row(mix_g), w_in_p, tiles["proj"])

    pad_g = lambda g: jnp.pad(g, (0, QK_PAD - QK_DIM)).reshape(1, QK_PAD)
    q, k, v = _mla_prep(cq, ckv, kr, pos, invf, row(q_g), _pad_rope_cols(w_uq).astype(_bf16), row(kv_g),
                        w_ukv.astype(_bf16), pad_g(qh_g), pad_g(kh_g), b, s, tiles["prep"])
    attn = _attention(q, k, v, tiles["attn"]).reshape(b * s, ATTN_W)

    rec = _rglru(xr, yg, conv_w, row(conv_b), w_rg.astype(_bf16), row(b_rg), w_ig.astype(_bf16), row(b_ig),
                 row(lam), b, s, tiles["rec"])

    x1, xn = _out_proj(x2, attn, rec, row(ao_g), row(ro_g), w_out.astype(_bf16), row(ffn_g), tiles["out"])

    e, g = _peer_topk(xn, w_q.astype(_bf16), k1.astype(_bf16), k2.astype(_bf16), tiles["topk"])
    wb = _peer_w(e, g, tiles["w"])
    return _peer_ffn(xn, x1, u_tab.T.astype(_bf16), v_tab.astype(_bf16), wb, tiles["ffn"], tiles["eblk"])


_TILES = dict(proj=256, prep=256, attn=1024, rec=256, out=256, topk=256, w=128, ffn=512, eblk=1024)


def _forward(tiles, x, positions, mix_norm_g, w_in, q_norm_g, w_uq, kv_norm_g, w_ukv, q_head_norm_g,
             k_head_norm_g, conv_w, conv_b, w_rgate, b_rgate, w_igate, b_igate, lru_lambda, attn_out_norm_g,
             rec_out_norm_g, w_out, ffn_norm_g, peer_w_q, peer_keys_1, peer_keys_2, peer_u, peer_v):
    b, s, d = x.shape
    half = ROPE // 2
    freq = ROPE_THETA ** (-jnp.arange(half, dtype=_f32) / half)
    invf = jnp.concatenate([freq, freq, jnp.zeros((LANES - ROPE,), _f32)]).reshape(1, LANES)
    pos = positions.reshape(b * s, 1)
    x2 = x.reshape(b * s, d)
    for l in range(mix_norm_g.shape[0]):
        x2 = _layer(x2, pos, invf, b, s, tiles, mix_norm_g[l], w_in[l], q_norm_g[l], w_uq[l], kv_norm_g[l],
                    w_ukv[l], q_head_norm_g[l], k_head_norm_g[l], conv_w[l], conv_b[l], w_rgate[l], b_rgate[l],
                    w_igate[l], b_igate[l], lru_lambda[l], attn_out_norm_g[l], rec_out_norm_g[l], w_out[l],
                    ffn_norm_g[l], peer_w_q[l], peer_keys_1[l], peer_keys_2[l], peer_u[l], peer_v[l])
    return x2.reshape(b, s, d)


def kernel(x, positions, mix_norm_g, w_in, q_norm_g, w_uq, kv_norm_g, w_ukv, q_head_norm_g, k_head_norm_g, conv_w, conv_b, w_rgate, b_rgate, w_igate, b_igate, lru_lambda, attn_out_norm_g, rec_out_norm_g, w_out, ffn_norm_g, peer_w_q, peer_keys_1, peer_keys_2, peer_u, peer_v):
    return _forward(_TILES, x, positions, mix_norm_g, w_in, q_norm_g, w_uq, kv_norm_g, w_ukv, q_head_norm_g,
                    k_head_norm_g, conv_w, conv_b, w_rgate, b_rgate, w_igate, b_igate, lru_lambda,
                    attn_out_norm_g, rec_out_norm_g, w_out, ffn_norm_g, peer_w_q, peer_keys_1, peer_keys_2,
                    peer_u, peer_v)
```

```python
import functools
import math

import jax
import jax.numpy as jnp
from jax import lax
from jax.experimental import pallas as pl
from jax.experimental.pallas import tpu as pltpu

EPS = 1e-6
LANES = 128
N_HEADS = 8
NOPE = 128
ROPE = 64
QK_DIM = NOPE + ROPE
QK_PAD = 256
V_DIM = 128
Q_RANK = 512
KV_RANK = 256
REC_W = 1024
ATTN_W = 1024
CONV_W = 4
LRU_C = 8.0
ROPE_THETA = 10000.0
TOPK = 16
N_KEYS = 128
PEER_HALF = 128
N_SLOTS = N_HEADS * TOPK
TOPK_COLS = 256
TOPK_CHAINS = 2
NEG = -1e30
VMEM_LIMIT = 56 * 1024 * 1024

_f32 = jnp.float32
_bf16 = jnp.bfloat16


def _cparams(sem):
    return pltpu.CompilerParams(dimension_semantics=sem, vmem_limit_bytes=VMEM_LIMIT)


def _rms(t, g):
    ms = jnp.mean(t * t, axis=-1, keepdims=True)
    return t * lax.rsqrt(ms + EPS) * g


def _gelu(t):
    return 0.5 * t * (1.0 + lax.erf(t * (1.0 / math.sqrt(2.0))))


def _proj_kernel(x_ref, g_ref, w_ref, cq_ref, ckv_ref, kr_ref, xr_ref, yg_ref):
    h = _rms(x_ref[...], g_ref[...]).astype(_bf16)

    def mm(lo, hi):
        return jnp.dot(h, w_ref[:, lo:hi], preferred_element_type=_f32)

    cq_ref[...] = mm(0, 512)
    ckv_ref[...] = mm(512, 768)
    kr_ref[...] = mm(768, 896)
    xr_ref[...] = mm(896, 1920)
    yg_ref[...] = mm(1920, 2944)


def _proj(x2, g, w_in_p, tile):
    n, d = x2.shape
    cols = w_in_p.shape[1]
    widths = (Q_RANK, KV_RANK, LANES, REC_W, REC_W)
    return pl.pallas_call(
        _proj_kernel,
        grid=(n // tile,),
        in_specs=[
            pl.BlockSpec((tile, d), lambda i: (i, 0)),
            pl.BlockSpec((1, d), lambda i: (0, 0)),
            pl.BlockSpec((d, cols), lambda i: (0, 0)),
        ],
        out_specs=[pl.BlockSpec((tile, w), lambda i: (i, 0)) for w in widths],
        out_shape=[jax.ShapeDtypeStruct((n, w), _f32) for w in widths],
        compiler_params=_cparams(("arbitrary",)),
        name="proj",
    )(x2, g, w_in_p)


def _mla_prep_kernel(cq_ref, ckv_ref, kr_ref, pos_ref, invf_ref, qg_ref, wuq_ref, kvg_ref, wukv_ref,
                     qhg_ref, khg_ref, q_ref, k_ref, v_ref):
    tile = cq_ref.shape[0]
    ang = pos_ref[...].astype(_f32) * invf_ref[...]
    cosv = jnp.cos(ang)
    sinv = jnp.sin(ang)
    lane = lax.broadcasted_iota(jnp.int32, (tile, LANES), 1)
    sin_signed = jnp.where(lane < ROPE // 2, -sinv, jnp.where(lane < ROPE, sinv, 0.0))

    def rope(t):
        swapped = jnp.where(lane < ROPE // 2, pltpu.roll(t, LANES - ROPE // 2, 1), pltpu.roll(t, ROPE // 2, 1))
        return t * cosv + swapped * sin_signed

    scale = QK_DIM ** -0.5 * math.log2(math.e)
    qf =jnp.dot(_rms(cq_ref[...], qg_ref[...]).astype(_bf16), wuq_ref[...], preferred_element_type=_f32)
    kvf = jnp.dot(_rms(ckv_ref[...], kvg_ref[...]).astype(_bf16), wukv_ref[...], preferred_element_type=_f32)
    qhg = qhg_ref[...]
    khg = khg_ref[...]
    kr = kr_ref[...]
    kr_ss = jnp.sum(kr * kr, axis=-1, keepdims=True)
    kr_base = rope(kr * khg[:, NOPE:])
    ones_col = jnp.where(lane == 0, 1.0, 0.0)
    for h in range(N_HEADS):
        qh = qf[:, h * QK_PAD:(h + 1) * QK_PAD]
        r = lax.rsqrt(jnp.sum(qh * qh, axis=-1, keepdims=True) * (1.0 / QK_DIM) + EPS) * scale
        qn = qh * r * qhg
        q_ref[0, h] = jnp.concatenate([qn[:, :NOPE], rope(qn[:, NOPE:])], axis=-1).astype(_bf16)
        kn = kvf[:, h * QK_PAD:h * QK_PAD + NOPE]
        rk = lax.rsqrt((jnp.sum(kn * kn, axis=-1, keepdims=True) + kr_ss) * (1.0 / QK_DIM) + EPS)
        k_ref[0, h] = jnp.concatenate([kn * rk * khg[:, :NOPE], kr_base * rk], axis=-1).astype(_bf16)
        v_ref[0, h] = jnp.concatenate([kvf[:, h * QK_PAD + NOPE:(h + 1) * QK_PAD], ones_col], axis=-1).astype(_bf16)


def _mla_prep(cq, ckv, kr, pos, invf, qg, wuq_p, kvg, wukv, qhg_p, khg_p, b, s, tile):
    nt = s // tile
    tok = lambda w: pl.BlockSpec((tile, w), lambda bi, ti: (bi * nt + ti, 0))
    full = lambda a: pl.BlockSpec(a.shape, lambda bi, ti: (0,) * a.ndim)
    head_out = lambda w: pl.BlockSpec((1, N_HEADS, tile, w), lambda bi, ti: (bi, 0, ti, 0))
    return pl.pallas_call(
        _mla_prep_kernel,
        grid=(b, nt),
        in_specs=[tok(Q_RANK), tok(KV_RANK), tok(LANES), tok(1), full(invf), full(qg), full(wuq_p), full(kvg),
                  full(wukv), full(qhg_p), full(khg_p)],
        out_specs=[head_out(QK_PAD), head_out(QK_PAD), head_out(V_PAD)],
        out_shape=[jax.ShapeDtypeStruct((b, N_HEADS, s, QK_PAD), _bf16),
                   jax.ShapeDtypeStruct((b, N_HEADS, s, QK_PAD), _bf16),
                   jax.ShapeDtypeStruct((b, N_HEADS, s, V_PAD), _bf16)],
        compiler_params=_cparams(("arbitrary", "arbitrary")),
        name="mla_prep",
    )(cq, ckv, kr, pos, invf, qg, wuq_p, kvg, wukv, qhg_p, khg_p)


ATT_ROWS = 256
ATT_CHUNK = 512
V_PAD = 256


def _attn_kernel(q_ref, k_ref, v_ref, o_ref, m_ref, acc_ref, sa_ref, sb_ref):
    tq = q_ref.shape[2]
    assert tq == 2 * ATT_CHUNK
    nsub = tq // ATT_ROWS
    qi = pl.program_id(2)
    n0 = 2 * qi
    m_ref[...] = jnp.full(m_ref.shape, NEG, _f32)
    acc_ref[...] = jnp.zeros(acc_ref.shape, _f32)

    def chunk(t):
        return pl.ds(pl.multiple_of(t * ATT_CHUNK, ATT_CHUNK), ATT_CHUNK)

    def scores(r, t, s_ref):
        q = q_ref[0, 0, r * ATT_ROWS:(r + 1) * ATT_ROWS, :]
        s_ref[r] = lax.dot_general(q, k_ref[0, 0, chunk(t), :], (((1,), (1,)), ((), ())),
                                   preferred_element_type=_f32)

    def update(r, t, s_ref, c=None):
        s = s_ref[r]
        if c is not None and (c + 1) * ATT_CHUNK - 1 > r * ATT_ROWS:
            row = lax.broadcasted_iota(jnp.int32, s.shape, 0) + r * ATT_ROWS
            col = lax.broadcasted_iota(jnp.int32, s.shape, 1) + c * ATT_CHUNK
            s = jnp.where(col <= row, s, NEG)
        m_prev = m_ref[r]
        m_new = jnp.maximum(m_prev, jnp.max(s, axis=-1, keepdims=True))
        alpha = jnp.exp2(m_prev - m_new)
        p = jnp.exp2(s - m_new[:, :1]).astype(_bf16)
        acc_ref[r] = jnp.concatenate([alpha, alpha], axis=-1) * acc_ref[r] + jnp.dot(
            p, v_ref[0, 0, chunk(t), :], preferred_element_type=_f32)
        m_ref[r] = m_new

    every = range(nsub)
    upper = [r for r in every if (r + 1) * ATT_ROWS > ATT_CHUNK]
    for r in every:
        scores(r, 0, sa_ref)

    def body(i, c):
        t = 2 * i
        for r in every:
            scores(r, t + 1, sb_ref)
        for r in every:
            update(r, t, sa_ref)
        for r in every:
            scores(r, t + 2, sa_ref)
        for r in every:
            update(r, t + 1, sb_ref)
        return c

    lax.fori_loop(0, qi, body, 0)
    for r in upper:
        scores(r, n0 + 1, sb_ref)
    for r in every:
        update(r, n0, sa_ref, c=0)
    for r in upper:
        update(r, n0 + 1, sb_ref, c=1)
    for r in range(nsub):
        acc = acc_ref[r]
        o_ref[0, r * ATT_ROWS:(r + 1) * ATT_ROWS, :] = acc[:, :V_DIM] / acc[:, V_DIM:V_DIM + 1]


def _attention(q, k, v, tq):
    b, h, s, _ = q.shape
    return pl.pallas_call(
        _attn_kernel,
        grid=(b, h, s // tq),
        in_specs=[
            pl.BlockSpec((1, 1, tq, QK_PAD), lambda bi, hi, qi: (bi, hi, qi, 0)),
            pl.BlockSpec((1, 1, s, QK_PAD), lambda bi, hi, qi: (bi, hi, 0, 0)),
            pl.BlockSpec((1, 1, s, V_PAD), lambda bi, hi, qi: (bi, hi, 0, 0)),
        ],
        out_specs=pl.BlockSpec((1, tq, V_DIM), lambda bi, hi, qi: (bi, qi, hi)),
        out_shape=jax.ShapeDtypeStruct((b, s, h * V_DIM), _f32),
        scratch_shapes=[pltpu.VMEM((tq // ATT_ROWS, ATT_ROWS, LANES), _f32),
                        pltpu.VMEM((tq // ATT_ROWS, ATT_ROWS, V_PAD), _f32),
                        pltpu.VMEM((tq // ATT_ROWS, ATT_ROWS, ATT_CHUNK), _f32),
                        pltpu.VMEM((tq // ATT_ROWS, ATT_ROWS, ATT_CHUNK), _f32)],
        compiler_params=_cparams(("arbitrary", "arbitrary", "arbitrary")),
        name="attn",
    )(q, k, v)


def _rglru_kernel(xr_ref, yg_ref, cw_ref, cb_ref, wr_ref, br_ref, wi_ref, bi_ref, lam_ref, o_ref,
                  ext_ref, h_ref):
    tile = xr_ref.shape[0]
    ti = pl.program_id(1)

    @pl.when(ti == 0)
    def _():
        ext_ref[0:8, :] = jnp.zeros((8, REC_W), _f32)
        h_ref[...] = jnp.zeros(h_ref.shape, _f32)

    x = xr_ref[...]
    ext_ref[8:, :] = x
    xc = cb_ref[...] + cw_ref[CONV_W - 1:CONV_W, :] * x
    for d in range(1, CONV_W):
        xc = xc + cw_ref[CONV_W - 1 - d:CONV_W - d, :] * ext_ref[8 - d:8 - d + tile, :]
    ext_ref[0:8, :] = x[tile - 8:, :]

    xcb = xc.astype(_bf16)
    rs, is_ = [], []
    for h in range(N_HEADS):
        xh = xcb[:, h * LANES:(h + 1) * LANES]
        rs.append(jnp.dot(xh, wr_ref[h], preferred_element_type=_f32))
        is_.append(jnp.dot(xh, wi_ref[h], preferred_element_type=_f32))
    r = jax.nn.sigmoid(jnp.concatenate(rs, axis=-1) + br_ref[...])
    i = jax.nn.sigmoid(jnp.concatenate(is_, axis=-1) + bi_ref[...])
    nl = -lam_ref[...]
    softplus = jnp.maximum(nl, 0.0) + jnp.log1p(jnp.exp(-jnp.abs(nl)))
    a = jnp.exp(-LRU_C * r * softplus)
    y = 1.0 - a * a
    bb = jnp.where(y > 0.0, y * lax.rsqrt(y), 0.0) * i * xc

    row = lax.broadcasted_iota(jnp.int32, (tile, REC_W), 0)
    d = 1
    while d < tile:
        keep = row >= d
        a_sh = jnp.where(keep, pltpu.roll(a, d, 0), 1.0)
        b_sh = jnp.where(keep, pltpu.roll(bb, d, 0), 0.0)
        bb = a * b_sh + bb
        a = a * a_sh
        d *= 2
    hh = bb + a * h_ref[0:1, :]
    h_ref[...] = jnp.broadcast_to(hh[tile - 1:tile, :], h_ref.shape)
    o_ref[...] = _gelu(yg_ref[...]) * hh


def _rglru(xr, yg, cw, cb, wr, br, wi, bi, lam, b, s, tile):
    nt = s // tile
    tok = pl.BlockSpec((tile, REC_W), lambda bi_, ti: (bi_ * nt + ti, 0))
    full = lambda a: pl.BlockSpec(a.shape, lambda bi_, ti: (0,) * a.ndim)
    return pl.pallas_call(
        _rglru_kernel,
        grid=(b, nt),
        in_specs=[tok, tok, full(cw), full(cb), full(wr), full(br), full(wi), full(bi), full(lam)],
        out_specs=tok,
        out_shape=jax.ShapeDtypeStruct((b * s, REC_W), _f32),
        scratch_shapes=[pltpu.VMEM((tile + 8, REC_W), _f32), pltpu.VMEM((8, REC_W), _f32)],
        compiler_params=_cparams(("arbitrary", "arbitrary")),
        name="rglru",
    )(xr, yg, cw, cb, wr, br, wi, bi, lam)


def _out_proj_kernel(x_ref, at_ref, rc_ref, ag_ref, rg_ref, wo_ref, fg_ref, x1_ref, xn_ref):
    an = _rms(at_ref[...], ag_ref[...]).astype(_bf16)
    rn = _rms(rc_ref[...], rg_ref[...]).astype(_bf16)
    y = jnp.dot(an, wo_ref[0:ATTN_W, :], preferred_element_type=_f32)
    y = y + jnp.dot(rn, wo_ref[ATTN_W:, :], preferred_element_type=_f32)
    x1 = x_ref[...] + y
    x1_ref[...] = x1
    xn_ref[...] = _rms(x1, fg_ref[...]).astype(_bf16)


def _out_proj(x2, attn, rec, ag, rg, wo, fg, tile):
    n, d = x2.shape
    tok = lambda w: pl.BlockSpec((tile, w), lambda i: (i, 0))
    full = lambda a: pl.BlockSpec(a.shape, lambda i: (0,) * a.ndim)
    return pl.pallas_call(
        _out_proj_kernel,
        grid=(n // tile,),
        in_specs=[tok(d), tok(ATTN_W), tok(REC_W), full(ag), full(rg), full(wo), full(fg)],
        out_specs=[tok(d), tok(d)],
        out_shape=[jax.ShapeDtypeStruct((n, d), _f32), jax.ShapeDtypeStruct((n, d), _bf16)],
        compiler_params=_cparams(("arbitrary",)),
        name="out_proj",
    )(x2, attn, rec, ag, rg, wo, fg)


def _extract_top(s, ids, big, payload=None):
    vals, sel, pay = [], [], []
    for _ in range(TOPK):
        m = jnp.max(s, axis=0, keepdims=True)
        i = jnp.min(jnp.where(s == m, ids, big), axis=0, keepdims=True)
        hit = ids == i
        vals.append(m)
        sel.append(i)
        if payload is not None:
            pay.append(jnp.max(jnp.where(hit, payload, -1), axis=0, keepdims=True))
        s = jnp.where(hit, NEG, s)
    return vals, sel, pay


CODE_ROWS = 16


def _extract_top_unique(scores, code):
    state = list(scores)
    res = [([], []) for _ in state]
    for _ in range(TOPK):
        for k, s in enumerate(state):
            m = jnp.max(s, axis=0, keepdims=True)
            hit = s == m
            res[k][0].append(m)
            res[k][1].append(jnp.dot(code, jnp.where(hit, 1.0, 0.0).astype(_bf16), preferred_element_type=_f32))
            state[k] = jnp.where(hit, NEG, s)
    return res


def _peer_topk_kernel(xn_ref, wq_ref, k1_ref, k2_ref, e_ref, g_ref, qs_ref, v1_ref, i1_ref, v2_ref, i2_ref,
                      tv_ref, pc_ref, qc_ref, et_ref, gt_ref):
    tile = xn_ref.shape[0]
    n_groups = N_HEADS // TOPK_CHAINS

    def group_queries(g):
        q = jnp.dot(xn_ref[...], wq_ref[g], preferred_element_type=_f32)
        return [q[:, c * 2 * PEER_HALF:(c + 1) * 2 * PEER_HALF].astype(_bf16) for c in range(TOPK_CHAINS)]

    def slot_rows(h):
        return pl.ds(h * TOPK if isinstance(h, int) else pl.multiple_of(h * TOPK, TOPK), TOPK)

    def project(g, slot):
        for c, qh in enumerate(group_queries(g)):
            qs_ref[slot, c] = qh
    key_id = lax.broadcasted_iota(jnp.int32, (N_KEYS, TOPK_COLS), 0)

    def cand_codes(row):
        low = row < TOPK
        mid = row < TOPK + 8 * 7
        cq = jnp.where(low, 0, jnp.where(mid, ((row - TOPK) >> 3) + 1, row - (TOPK + 8 * 7) + 8))
        cp = jnp.where(low, row, jnp.where(mid, (row - TOPK) & 7, 0))
        ok = ((cp + 1) * (cq + 1) <= TOPK) & (row < TOPK + 8 * 8)
        return cp, cq, ok

    n_rows = N_KEYS
    cp, cq, cand_ok = cand_codes(lax.broadcasted_iota(jnp.int32, (n_rows, TOPK_COLS), 0))
    cand_id = cp * TOPK + cq
    code_row = lax.broadcasted_iota(jnp.int32, (CODE_ROWS, n_rows), 0)
    code_col = lax.broadcasted_iota(jnp.int32, (CODE_ROWS, n_rows), 1)
    key_code = jnp.where(code_row == 0, code_col, jnp.where(code_row == 1, 1, 0)).astype(_f32).astype(_bf16)
    lp, lq, _ = cand_codes(code_col)
    pair_code = jnp.where(code_row == 0, lp, jnp.where(code_row == 1, lq, jnp.where(code_row == 2, 1, 0)))
    pair_code = pair_code.astype(_f32).astype(_bf16)

    def half_scores(h, qh, cols, half):
        kref = (k1_ref, k2_ref)[half]
        return lax.dot_general(kref[h], qh[cols, half * PEER_HALF:(half + 1) * PEER_HALF],
                               (((1,), (1,)), ((), ())), preferred_element_type=_f32)

    def candidates(c, cols):
        v1 = v1_ref[c, :, cols]
        i1 = i1_ref[c, :, cols] * N_KEYS
        cv = [v1 + v2_ref[c, 0:1, cols]]
        ce = [i1 + i2_ref[c, 0:1, cols]]
        for qq in range(1, 8):
            cv.append(v1[0:8] + v2_ref[c, qq:qq + 1, cols])
            ce.append(i1[0:8] + i2_ref[c, qq:qq + 1, cols])
        cv.append(v1[0:1] + v2_ref[c, 8:16, cols])
        ce.append(i1[0:1] + i2_ref[c, 8:16, cols])
        pad = n_rows - (TOPK + 8 * 8)
        cv.append(jnp.full((pad, TOPK_COLS), NEG, _f32))
        ce.append(jnp.zeros((pad, TOPK_COLS), jnp.int32))
        return jnp.where(cand_ok, jnp.concatenate(cv, axis=0), NEG), jnp.concatenate(ce, axis=0)

    def gates(c, h):
        tv = tv_ref[c]
        ex = jnp.exp(tv - tv[0:1, :])
        gt_ref[slot_rows(h), :] = ex / jnp.sum(ex, axis=0, keepdims=True)

    def head_group(i, slot, most_hits):
        heads = [(c, i * TOPK_CHAINS + c) for c in range(TOPK_CHAINS)]
        for lt in range(tile // TOPK_COLS):
            cols = slice(lt * TOPK_COLS, (lt + 1) * TOPK_COLS)
            halves = [(c, h, half) for c, h in heads for half in range(2)]
            found = _extract_top_unique([half_scores(h, qs_ref[slot, c], cols, half) for c, h, half in halves],
                                        key_code)
            for (c, h, half), (vals, outs) in zip(halves, found):
                vref, iref = ((v1_ref, i1_ref), (v2_ref, i2_ref))[half]
                for kk in range(TOPK):
                    vref[c, kk:kk + 1, cols] = vals[kk]
                    iref[c, kk:kk + 1, cols] = outs[kk][0:1].astype(jnp.int32)
                    most_hits = jnp.maximum(most_hits, outs[kk][1:2])
            found = _extract_top_unique([candidates(c, cols)[0] for c, _ in heads], pair_code)
            for (c, h), (top, outs) in zip(heads, found):
                for kk in range(TOPK):
                    tv_ref[c, kk:kk + 1, cols] = top[kk]
                    pc_ref[c, kk:kk + 1, cols] = outs[kk][0:1]
                    qc_ref[c, kk:kk + 1, cols] = outs[kk][1:2]
                    most_hits = jnp.maximum(most_hits, outs[kk][2:3])
                pc = pc_ref[c, :, cols]
                qc = qc_ref[c, :, cols]
                e1 = jnp.zeros((TOPK, TOPK_COLS), jnp.int32)
                e2 = jnp.zeros((TOPK, TOPK_COLS), jnp.int32)
                for r in range(TOPK):
                    e1 = jnp.where(pc == float(r), i1_ref[c, r:r + 1, cols], e1)
                    e2 = jnp.where(qc == float(r), i2_ref[c, r:r + 1, cols], e2)
                et_ref[lt, slot_rows(h), :] = e1 * N_KEYS + e2
        for c, h in heads:
            gates(c, h)
        return most_hits

    def group_general(g, carry):
        for c, qh in enumerate(group_queries(g)):
            h = g * TOPK_CHAINS + c
            for lt in range(tile // TOPK_COLS):
                cols = slice(lt * TOPK_COLS, (lt + 1) * TOPK_COLS)
                for half, (vref, iref) in enumerate(((v1_ref, i1_ref), (v2_ref, i2_ref))):
                    vals, sel, _ = _extract_top(half_scores(h, qh, cols, half), key_id, N_KEYS)
                    for kk in range(TOPK):
                        vref[c, kk:kk + 1, cols] = vals[kk]
                        iref[c, kk:kk + 1, cols] = sel[kk]
                cv, ce = candidates(c, cols)
                top, _, picked = _extract_top(cv, cand_id, TOPK * TOPK, payload=ce)
                for kk in range(TOPK):
                    tv_ref[c, kk:kk + 1, cols] = top[kk]
                    et_ref[lt, pl.ds(h * TOPK + kk, 1), :] = picked[kk]
            gates(c, h)
        return carry

    most_hits = jnp.zeros((1, TOPK_COLS), _f32)
    project(0, 0)
    for g in range(n_groups):
        if g + 1 < n_groups:
            project(g + 1, (g + 1) % 2)
        most_hits = head_group(g, g % 2, most_hits)

    @pl.when(jnp.max(most_hits) > 1.5)
    def _():
        lax.fori_loop(0, n_groups, group_general, 0)

    for lt in range(tile // TOPK_COLS):
        e_ref[lt * TOPK_COLS:(lt + 1) * TOPK_COLS, :] = et_ref[lt].T
    g_ref[...] = gt_ref[...].T


def _peer_topk(xn, wq_h, k1, k2, tile):
    n, d = xn.shape
    full = lambda a: pl.BlockSpec(a.shape, lambda i: (0,) * a.ndim)
    slot = pl.BlockSpec((tile, N_SLOTS), lambda i: (i, 0))
    return pl.pallas_call(
        _peer_topk_kernel,
        grid=(n // tile,),
        in_specs=[pl.BlockSpec((tile, d), lambda i: (i, 0)), full(wq_h), full(k1), full(k2)],
        out_specs=[slot, slot],
        out_shape=[jax.ShapeDtypeStruct((n, N_SLOTS), jnp.int32), jax.ShapeDtypeStruct((n, N_SLOTS), _f32)],
        scratch_shapes=[pltpu.VMEM((2, TOPK_CHAINS, tile, 2 * PEER_HALF), _bf16),
                        pltpu.VMEM((TOPK_CHAINS, TOPK, tile), _f32), pltpu.VMEM((TOPK_CHAINS, TOPK, tile), jnp.int32),
                        pltpu.VMEM((TOPK_CHAINS, TOPK, tile), _f32), pltpu.VMEM((TOPK_CHAINS, TOPK, tile), jnp.int32),
                        pltpu.VMEM((TOPK_CHAINS, TOPK, tile), _f32), pltpu.VMEM((TOPK_CHAINS, TOPK, tile), _f32),
                        pltpu.VMEM((TOPK_CHAINS, TOPK, tile), _f32),
                        pltpu.VMEM((tile // TOPK_COLS, N_SLOTS, TOPK_COLS), jnp.int32),
                        pltpu.VMEM((N_SLOTS, tile), _f32)],
        compiler_params=_cparams(("arbitrary",)),
        name="peer_topk",
    )(xn, wq_h, k1, k2)


W_PITCH = N_KEYS + 8
W_GROUP = 8
W_UNROLL = 8


def _peer_w_kernel(e_ref, g_ref, w_ref, scr_ref):
    tile = e_ref.shape[0]
    sub = lax.broadcasted_iota(jnp.int32, (N_KEYS, N_SLOTS), 0)

    def group(i, c):
        for sg in range(W_UNROLL):
            tokens8(pl.multiple_of((i * W_UNROLL + sg) * W_GROUP, W_GROUP))
        return c

    def tokens8(base):
        e8 = e_ref[pl.ds(base, W_GROUP), :]
        g8 = g_ref[pl.ds(base, W_GROUP), :]
        zero = jnp.zeros((N_KEYS, N_SLOTS), _bf16)
        for t in range(0, W_GROUP, 2):
            ga, hb = [], []
            for u in (t, t + 1):
                e = e8[u:u + 1, :]
                ga.append(jnp.where(sub == (e >> 7), g8[u:u + 1, :], 0.0).astype(_bf16))
                hb.append(jnp.where(sub == (e & (N_KEYS - 1)), 1.0, 0.0).astype(_bf16))
            lhs = jnp.concatenate(ga, axis=1)
            rhs = jnp.concatenate([jnp.concatenate([hb[0], zero], axis=1),
                                   jnp.concatenate([zero, hb[1]], axis=1)], axis=0)
            w2 = lax.dot_general(lhs, rhs, (((1,), (1,)), ((), ())), preferred_element_type=_f32)
            for k, u in enumerate((t, t + 1)):
                scr_ref[pl.ds(pl.multiple_of((base + u) * W_PITCH, 8), N_KEYS), :] = w2[:, k * N_KEYS:(k + 1) * N_KEYS]

    lax.fori_loop(0, tile // (W_GROUP * W_UNROLL), group, 0)

    def regroup(i, c):
        for t in range(4):
            r = i * 4 + t
            w_ref[r] = scr_ref[pl.ds(r, tile, stride=W_PITCH), :].astype(_bf16)
        return c

    lax.fori_loop(0, N_KEYS // 4, regroup, 0)


def _peer_w(e, g, tile):
    n = e.shape[0]
    slot = pl.BlockSpec((tile, N_SLOTS), lambda i: (i, 0))
    return pl.pallas_call(
        _peer_w_kernel,
        grid=(n // tile,),
        in_specs=[slot, slot],
        out_specs=pl.BlockSpec((N_KEYS, tile, N_KEYS), lambda i: (0, i, 0)),
        out_shape=jax.ShapeDtypeStruct((N_KEYS, n, N_KEYS), _bf16),
        scratch_shapes=[pltpu.VMEM((tile * W_PITCH, N_KEYS), _f32)],
        compiler_params=_cparams(("arbitrary",)),
        name="peer_w",
    )(e, g)


def _peer_ffn_kernel(nblk, xn_ref, x1_ref, ut_ref, v_ref, w_ref, o_ref, pa_ref, pb_ref):
    j = pl.program_id(1)
    rows = w_ref.shape[0]

    @pl.when(j == 0)
    def _():
        o_ref[...] = x1_ref[...]
        pb_ref[...] = jnp.zeros(pb_ref.shape, _bf16)

    def step(p_new_ref, p_prev_ref):
        if p_new_ref is not None:
            a = jnp.dot(xn_ref[...], ut_ref[...], preferred_element_type=_f32)
            w = jnp.concatenate([w_ref[r] for r in range(rows)], axis=-1).astype(_f32)
            p_new_ref[...] = (_gelu(a) * w).astype(_bf16)
        o_ref[...] += jnp.dot(p_prev_ref[...], v_ref[...], preferred_element_type=_f32)

    last = nblk
    assert nblk % 2 == 0

    @pl.when((lax.rem(j, 2) == 0) & (j < last))
    def _():
        step(pa_ref, pb_ref)

    @pl.when(lax.rem(j, 2) == 1)
    def _():
        step(pb_ref, pa_ref)

    @pl.when(j == last)
    def _():
        step(None, pb_ref)


def _peer_ffn(xn, x1, ut, v, wb, tile, eblk):
    n, d = xn.shape
    nblk = v.shape[0] // eblk
    rows = eblk // N_KEYS
    cur = lambda j: jnp.minimum(j, nblk - 1)
    prev = lambda j: jnp.maximum(j - 1, 0)
    return pl.pallas_call(
        functools.partial(_peer_ffn_kernel, nblk),
        grid=(n // tile, nblk + 1),
        in_specs=[
            pl.BlockSpec((tile, d), lambda i, j: (i, 0), pipeline_mode=pl.Buffered(1)),
            pl.BlockSpec((tile, d), lambda i, j: (i, 0), pipeline_mode=pl.Buffered(1)),
            pl.BlockSpec((d, eblk), lambda i, j: (0, cur(j))),
            pl.BlockSpec((eblk, d), lambda i, j: (prev(j), 0)),
            pl.BlockSpec((rows, tile, N_KEYS), lambda i, j: (cur(j), i, 0)),
        ],
        out_specs=pl.BlockSpec((tile, d), lambda i, j: (i, 0)),
        out_shape=jax.ShapeDtypeStruct((n, d), _f32),
        scratch_shapes=[pltpu.VMEM((tile, eblk), _bf16), pltpu.VMEM((tile, eblk), _bf16)],
        compiler_params=_cparams(("arbitrary", "arbitrary")),
        name="peer_ffn",
    )(xn, x1, ut, v, wb)


def _pad_rope_cols(w):
    lead = w.shape[:-1]
    w = w.reshape(*lead, N_HEADS, QK_DIM)
    w = jnp.pad(w, [(0, 0)] * len(lead) + [(0, 0), (0, QK_PAD - QK_DIM)])
    return w.reshape(*lead, N_HEADS * QK_PAD)


def _layer(x2, pos, invf, b, s, tiles, mix_g, w_in, q_g, w_uq, kv_g, w_ukv, qh_g, kh_g, conv_w, conv_b, w_rg,
           b_rg, w_ig, b_ig, lam, ao_g, ro_g, w_out, ffn_g, w_q, k1, k2, u_tab, v_tab):
    d = x2.shape[1]
    row = lambda a: a.reshape(1, -1)
    off_kr = Q_RANK + KV_RANK
    off_xr = off_kr + ROPE
    w_in_p = jnp.concatenate(
        [w_in[:, :off_xr], jnp.zeros((d, LANES - ROPE), w_in.dtype), w_in[:, off_xr:]], axis=1).astype(_bf16)
    cq, ckv, kr, xr, yg = _proj(x2, row(mix_g), w_in_p, tiles["proj"])

    pad_g = lambda g: jnp.pad(g, (0, QK_PAD - QK_DIM)).reshape(1, QK_PAD)
    q, k, v = _mla_prep(cq, ckv, kr, pos, invf, row(q_g), _pad_rope_cols(w_uq).astype(_bf16), row(kv_g),
                        w_ukv.astype(_bf16), pad_g(qh_g), pad_g(kh_g), b, s, tiles["prep"])
    attn = _attention(q, k, v, tiles["attn"]).reshape(b * s, ATTN_W)

    rec = _rglru(xr, yg, conv_w, row(conv_b), w_rg.astype(_bf16), row(b_rg), w_ig.astype(_bf16), row(b_ig),
                 row(lam), b, s, tiles["rec"])

    x1, xn = _out_proj(x2, attn, rec, row(ao_g), row(ro_g), w_out.astype(_bf16), row(ffn_g), tiles["out"])

    wq_g = w_q.reshape(d, N_HEADS // TOPK_CHAINS, -1).transpose(1, 0, 2).astype(_bf16)
    e, g = _peer_topk(xn, wq_g, k1.astype(_bf16), k2.astype(_bf16), tiles["topk"])
    wb = _peer_w(e, g, tiles["w"])
    return _peer_ffn(xn, x1, u_tab.T.astype(_bf16), v_tab.astype(_bf16), wb, tiles["ffn"], tiles["eblk"])


_TILES = dict(proj=256, prep=256, attn=1024, rec=256, out=256, topk=256, w=128, ffn=1024, eblk=512)


def _forward(tiles, x, positions, mix_norm_g, w_in, q_norm_g, w_uq, kv_norm_g, w_ukv, q_head_norm_g,
             k_head_norm_g, conv_w, conv_b, w_rgate, b_rgate, w_igate, b_igate, lru_lambda, attn_out_norm_g,
             rec_out_norm_g, w_out, ffn_norm_g, peer_w_q, peer_keys_1, peer_keys_2, peer_u, peer_v):
    b, s, d = x.shape
    half = ROPE // 2
    freq = ROPE_THETA ** (-jnp.arange(half, dtype=_f32) / half)
    invf = jnp.concatenate([freq, freq, jnp.zeros((LANES - ROPE,), _f32)]).reshape(1, LANES)
    pos = positions.reshape(b * s, 1)
    x2 = x.reshape(b * s, d)
    for l in range(mix_norm_g.shape[0]):
        x2 = _layer(x2, pos, invf, b, s, tiles, mix_norm_g[l], w_in[l], q_norm_g[l], w_uq[l], kv_norm_g[l],
                    w_ukv[l], q_head_norm_g[l], k_head_norm_g[l], conv_w[l], conv_b[l], w_rgate[l], b_rgate[l],
                    w_igate[l], b_igate[l], lru_lambda[l], attn_out_norm_g[l], rec_out_norm_g[l], w_out[l],
                    ffn_norm_g[l], peer_w_q[l], peer_keys_1[l], peer_keys_2[l], peer_u[l], peer_v[l])
    return x2.reshape(b, s, d)


def kernel(x, positions, mix_norm_g, w_in, q_norm_g, w_uq, kv_norm_g, w_ukv, q_head_norm_g, k_head_norm_g, conv_w, conv_b, w_rgate, b_rgate, w_igate, b_igate, lru_lambda, attn_out_norm_g, rec_out_norm_g, w_out, ffn_norm_g, peer_w_q, peer_keys_1, peer_keys_2, peer_u, peer_v):
    return _forward(_TILES, x, positions, mix_norm_g, w_in, q_norm_g, w_uq, kv_norm_g, w_ukv, q_head_norm_g,
                    k_head_norm_g, conv_w, conv_b, w_rgate, b_rgate, w_igate, b_igate, lru_lambda,
                    attn_out_norm_g, rec_out_norm_g, w_out, ffn_norm_g, peer_w_q, peer_keys_1, peer_keys_2,
                    peer_u, peer_v)
```

```python
import functools
import math

import jax
import jax.numpy as jnp
from jax import lax
from jax.experimental import pallas as pl
from jax.experimental.pallas import tpu as pltpu

EPS = 1e-6
LANES = 128
N_HEADS = 8
NOPE = 128
ROPE = 64
QK_DIM = NOPE + ROPE
QK_PAD = 256
V_DIM = 128
Q_RANK = 512
KV_RANK = 256
REC_W = 1024
ATTN_W = 1024
CONV_W = 4
LRU_C = 8.0
ROPE_THETA = 10000.0
TOPK = 16
N_KEYS = 128
PEER_HALF = 128
N_SLOTS = N_HEADS * TOPK
TOPK_COLS = 256
TOPK_CHAINS = 2
NEG = -1e30
VMEM_LIMIT = 56 * 1024 * 1024

_f32 = jnp.float32
_bf16 = jnp.bfloat16


def _cparams(sem):
    return pltpu.CompilerParams(dimension_semantics=sem, vmem_limit_bytes=VMEM_LIMIT)


def _rms(t, g):
    ms = jnp.mean(t * t, axis=-1, keepdims=True)
    return t * lax.rsqrt(ms + EPS) * g


def _gelu(t):
    return 0.5 * t * (1.0 + lax.erf(t * (1.0 / math.sqrt(2.0))))


def _proj_kernel(x_ref, g_ref, w_ref, cq_ref, ckv_ref, kr_ref, xr_ref, yg_ref):
    h = _rms(x_ref[...], g_ref[...]).astype(_bf16)

    def mm(lo, hi):
        return jnp.dot(h, w_ref[:, lo:hi], preferred_element_type=_f32)

    cq_ref[...] = mm(0, 512)
    ckv_ref[...] = mm(512, 768)
    kr_ref[...] = mm(768, 896)
    xr_ref[...] = mm(896, 1920)
    yg_ref[...] = mm(1920, 2944)


def _proj(x2, g, w_in_p, tile):
    n, d = x2.shape
    cols = w_in_p.shape[1]
    widths = (Q_RANK, KV_RANK, LANES, REC_W, REC_W)
    return pl.pallas_call(
        _proj_kernel,
        grid=(n // tile,),
        in_specs=[
            pl.BlockSpec((tile, d), lambda i: (i, 0)),
            pl.BlockSpec((1, d), lambda i: (0, 0)),
            pl.BlockSpec((d, cols), lambda i: (0, 0)),
        ],
        out_specs=[pl.BlockSpec((tile, w), lambda i: (i, 0)) for w in widths],
        out_shape=[jax.ShapeDtypeStruct((n, w), _f32) for w in widths],
        compiler_params=_cparams(("arbitrary",)),
        name="proj",
    )(x2, g, w_in_p)


def _mla_prep_kernel(cq_ref, ckv_ref, kr_ref, pos_ref, invf_ref, qg_ref, wuq_ref, kvg_ref, wukv_ref,
                     qhg_ref, khg_ref, q_ref, k_ref, v_ref):
    tile = cq_ref.shape[0]
    ang = pos_ref[...].astype(_f32) * invf_ref[...]
    cosv = jnp.cos(ang)
    sinv = jnp.sin(ang)
    lane = lax.broadcasted_iota(jnp.int32, (tile, LANES), 1)
    sin_signed = jnp.where(lane < ROPE // 2, -sinv, jnp.where(lane < ROPE, sinv, 0.0))

    def rope(t):
        swapped = jnp.where(lane < ROPE // 2, pltpu.roll(t, LANES - ROPE // 2, 1), pltpu.roll(t, ROPE // 2, 1))
        return t * cosv + swapped * sin_signed

    scale = QK_DIM ** -0.5 * math.log2(math.e)
    qf =jnp.dot(_rms(cq_ref[...], qg_ref[...]).astype(_bf16), wuq_ref[...], preferred_element_type=_f32)
    kvf = jnp.dot(_rms(ckv_ref[...], kvg_ref[...]).astype(_bf16), wukv_ref[...], preferred_element_type=_f32)
    qhg = qhg_ref[...]
    khg = khg_ref[...]
    kr = kr_ref[...]
    kr_ss = jnp.sum(kr * kr, axis=-1, keepdims=True)
    kr_base = rope(kr * khg[:, NOPE:])
    ones_col = jnp.where(lane == 0, 1.0, 0.0)
    for h in range(N_HEADS):
        qh = qf[:, h * QK_PAD:(h + 1) * QK_PAD]
        r = lax.rsqrt(jnp.sum(qh * qh, axis=-1, keepdims=True) * (1.0 / QK_DIM) + EPS) * scale
        qn = qh * r * qhg
        q_ref[0, h] = jnp.concatenate([qn[:, :NOPE], rope(qn[:, NOPE:])], axis=-1).astype(_bf16)
        kn = kvf[:, h * QK_PAD:h * QK_PAD + NOPE]
        rk = lax.rsqrt((jnp.sum(kn * kn, axis=-1, keepdims=True) + kr_ss) * (1.0 / QK_DIM) + EPS)
        k_ref[0, h] = jnp.concatenate([kn * rk * khg[:, :NOPE], kr_base * rk], axis=-1).astype(_bf16)
        v_ref[0, h] = jnp.concatenate([kvf[:, h * QK_PAD + NOPE:(h + 1) * QK_PAD], ones_col], axis=-1).astype(_bf16)


def _mla_prep(cq, ckv, kr, pos, invf, qg, wuq_p, kvg, wukv, qhg_p, khg_p, b, s, tile):
    nt = s // tile
    tok = lambda w: pl.BlockSpec((tile, w), lambda bi, ti: (bi * nt + ti, 0))
    full = lambda a: pl.BlockSpec(a.shape, lambda bi, ti: (0,) * a.ndim)
    head_out = lambda w: pl.BlockSpec((1, N_HEADS, tile, w), lambda bi, ti: (bi, 0, ti, 0))
    return pl.pallas_call(
        _mla_prep_kernel,
        grid=(b, nt),
        in_specs=[tok(Q_RANK), tok(KV_RANK), tok(LANES), tok(1), full(invf), full(qg), full(wuq_p), full(kvg),
                  full(wukv), full(qhg_p), full(khg_p)],
        out_specs=[head_out(QK_PAD), head_out(QK_PAD), head_out(V_PAD)],
        out_shape=[jax.ShapeDtypeStruct((b, N_HEADS, s, QK_PAD), _bf16),
                   jax.ShapeDtypeStruct((b, N_HEADS, s, QK_PAD), _bf16),
                   jax.ShapeDtypeStruct((b, N_HEADS, s, V_PAD), _bf16)],
        compiler_params=_cparams(("arbitrary", "arbitrary")),
        name="mla_prep",
    )(cq, ckv, kr, pos, invf, qg, wuq_p, kvg, wukv, qhg_p, khg_p)


ATT_ROWS = 256
ATT_CHUNK = 1024
V_PAD = 256


def _attn_kernel(q_ref, k_ref, v_ref, o_ref, m_ref, acc_ref, sa_ref, sb_ref):
    tq = q_ref.shape[2]
    assert tq == 2 * ATT_CHUNK
    nsub = tq // ATT_ROWS
    qi = pl.program_id(2)
    n0 = 2 * qi
    m_ref[...] = jnp.full(m_ref.shape, NEG, _f32)
    acc_ref[...] = jnp.zeros(acc_ref.shape, _f32)

    def chunk(t):
        return pl.ds(pl.multiple_of(t * ATT_CHUNK, ATT_CHUNK), ATT_CHUNK)

    def scores(r, t, s_ref):
        q = q_ref[0, 0, r * ATT_ROWS:(r + 1) * ATT_ROWS, :]
        s_ref[r] = lax.dot_general(q, k_ref[0, 0, chunk(t), :], (((1,), (1,)), ((), ())),
                                   preferred_element_type=_f32)

    def update(r, t, s_ref, c=None):
        s = s_ref[r]
        if c is not None and (c + 1) * ATT_CHUNK - 1 > r * ATT_ROWS:
            row = lax.broadcasted_iota(jnp.int32, s.shape, 0) + r * ATT_ROWS
            col = lax.broadcasted_iota(jnp.int32, s.shape, 1) + c * ATT_CHUNK
            s = jnp.where(col <= row, s, NEG)
        m_prev = m_ref[r]
        m_new = jnp.maximum(m_prev, jnp.max(s, axis=-1, keepdims=True))
        alpha = jnp.exp2(m_prev - m_new)
        p = jnp.exp2(s - m_new[:, :1]).astype(_bf16)
        acc_ref[r] = jnp.concatenate([alpha, alpha], axis=-1) * acc_ref[r] + jnp.dot(
            p, v_ref[0, 0, chunk(t), :], preferred_element_type=_f32)
        m_ref[r] = m_new

    every = range(nsub)
    upper = [r for r in every if (r + 1) * ATT_ROWS > ATT_CHUNK]
    for r in every:
        scores(r, 0, sa_ref)

    def body(i, c):
        t = 2 * i
        for r in every:
            scores(r, t + 1, sb_ref)
        for r in every:
            update(r, t, sa_ref)
        for r in every:
            scores(r, t + 2, sa_ref)
        for r in every:
            update(r, t + 1, sb_ref)
        return c

    lax.fori_loop(0, qi, body, 0)
    for r in upper:
        scores(r, n0 + 1, sb_ref)
    for r in every:
        update(r, n0, sa_ref, c=0)
    for r in upper:
        update(r, n0 + 1, sb_ref, c=1)
    for r in range(nsub):
        acc = acc_ref[r]
        o_ref[0, r * ATT_ROWS:(r + 1) * ATT_ROWS, :] = acc[:, :V_DIM] / acc[:, V_DIM:V_DIM + 1]


def _attention(q, k, v, tq):
    b, h, s, _ = q.shape
    return pl.pallas_call(
        _attn_kernel,
        grid=(b, h, s // tq),
        in_specs=[
            pl.BlockSpec((1, 1, tq, QK_PAD), lambda bi, hi, qi: (bi, hi, qi, 0)),
            pl.BlockSpec((1, 1, s, QK_PAD), lambda bi, hi, qi: (bi, hi, 0, 0)),
            pl.BlockSpec((1, 1, s, V_PAD), lambda bi, hi, qi: (bi, hi, 0, 0)),
        ],
        out_specs=pl.BlockSpec((1, tq, V_DIM), lambda bi, hi, qi: (bi, qi, hi)),
        out_shape=jax.ShapeDtypeStruct((b, s, h * V_DIM), _f32),
        scratch_shapes=[pltpu.VMEM((tq // ATT_ROWS, ATT_ROWS, LANES), _f32),
                        pltpu.VMEM((tq // ATT_ROWS, ATT_ROWS, V_PAD), _f32),
                        pltpu.VMEM((tq // ATT_ROWS, ATT_ROWS, ATT_CHUNK), _f32),
                        pltpu.VMEM((tq // ATT_ROWS, ATT_ROWS, ATT_CHUNK), _f32)],
        compiler_params=_cparams(("arbitrary", "arbitrary", "arbitrary")),
        name="attn",
    )(q, k, v)


def _rglru_kernel(xr_ref, yg_ref, cw_ref, cb_ref, wr_ref, br_ref, wi_ref, bi_ref, lam_ref, o_ref,
                  ext_ref, h_ref):
    tile = xr_ref.shape[0]
    ti = pl.program_id(1)

    @pl.when(ti == 0)
    def _():
        ext_ref[0:8, :] = jnp.zeros((8, REC_W), _f32)
        h_ref[...] = jnp.zeros(h_ref.shape, _f32)

    x = xr_ref[...]
    ext_ref[8:, :] = x
    xc = cb_ref[...] + cw_ref[CONV_W - 1:CONV_W, :] * x
    for d in range(1, CONV_W):
        xc = xc + cw_ref[CONV_W - 1 - d:CONV_W - d, :] * ext_ref[8 - d:8 - d + tile, :]
    ext_ref[0:8, :] = x[tile - 8:, :]

    xcb = xc.astype(_bf16)
    rs, is_ = [], []
    for h in range(N_HEADS):
        xh = xcb[:, h * LANES:(h + 1) * LANES]
        rs.append(jnp.dot(xh, wr_ref[h], preferred_element_type=_f32))
        is_.append(jnp.dot(xh, wi_ref[h], preferred_element_type=_f32))
    r = jax.nn.sigmoid(jnp.concatenate(rs, axis=-1) + br_ref[...])
    i = jax.nn.sigmoid(jnp.concatenate(is_, axis=-1) + bi_ref[...])
    nl = -lam_ref[...]
    softplus = jnp.maximum(nl, 0.0) + jnp.log1p(jnp.exp(-jnp.abs(nl)))
    a = jnp.exp(-LRU_C * r * softplus)
    y = 1.0 - a * a
    bb = jnp.where(y > 0.0, y * lax.rsqrt(y), 0.0) * i * xc

    row = lax.broadcasted_iota(jnp.int32, (tile, REC_W), 0)
    d = 1
    while d < tile:
        keep = row >= d
        a_sh = jnp.where(keep, pltpu.roll(a, d, 0), 1.0)
        b_sh = jnp.where(keep, pltpu.roll(bb, d, 0), 0.0)
        bb = a * b_sh + bb
        a = a * a_sh
        d *= 2
    hh = bb + a * h_ref[0:1, :]
    h_ref[...] = jnp.broadcast_to(hh[tile - 1:tile, :], h_ref.shape)
    o_ref[...] = _gelu(yg_ref[...]) * hh


def _rglru(xr, yg, cw, cb, wr, br, wi, bi, lam, b, s, tile):
    nt = s // tile
    tok = pl.BlockSpec((tile, REC_W), lambda bi_, ti: (bi_ * nt + ti, 0))
    full = lambda a: pl.BlockSpec(a.shape, lambda bi_, ti: (0,) * a.ndim)
    return pl.pallas_call(
        _rglru_kernel,
        grid=(b, nt),
        in_specs=[tok, tok, full(cw), full(cb), full(wr), full(br), full(wi), full(bi), full(lam)],
        out_specs=tok,
        out_shape=jax.ShapeDtypeStruct((b * s, REC_W), _f32),
        scratch_shapes=[pltpu.VMEM((tile + 8, REC_W), _f32), pltpu.VMEM((8, REC_W), _f32)],
        compiler_params=_cparams(("arbitrary", "arbitrary")),
        name="rglru",
    )(xr, yg, cw, cb, wr, br, wi, bi, lam)


def _out_proj_kernel(x_ref, at_ref, rc_ref, ag_ref, rg_ref, wo_ref, fg_ref, x1_ref, xn_ref):
    an = _rms(at_ref[...], ag_ref[...]).astype(_bf16)
    rn = _rms(rc_ref[...], rg_ref[...]).astype(_bf16)
    y = jnp.dot(an, wo_ref[0:ATTN_W, :], preferred_element_type=_f32)
    y = y + jnp.dot(rn, wo_ref[ATTN_W:, :], preferred_element_type=_f32)
    x1 = x_ref[...] + y
    x1_ref[...] = x1
    xn_ref[...] = _rms(x1, fg_ref[...]).astype(_bf16)


def _out_proj(x2, attn, rec, ag, rg, wo, fg, tile):
    n, d = x2.shape
    tok = lambda w: pl.BlockSpec((tile, w), lambda i: (i, 0))
    full = lambda a: pl.BlockSpec(a.shape, lambda i: (0,) * a.ndim)
    return pl.pallas_call(
        _out_proj_kernel,
        grid=(n // tile,),
        in_specs=[tok(d), tok(ATTN_W), tok(REC_W), full(ag), full(rg), full(wo), full(fg)],
        out_specs=[tok(d), tok(d)],
        out_shape=[jax.ShapeDtypeStruct((n, d), _f32), jax.ShapeDtypeStruct((n, d), _bf16)],
        compiler_params=_cparams(("arbitrary",)),
        name="out_proj",
    )(x2, attn, rec, ag, rg, wo, fg)


def _extract_top(s, ids, big, payload=None):
    vals, sel, pay = [], [], []
    for _ in range(TOPK):
        m = jnp.max(s, axis=0, keepdims=True)
        i = jnp.min(jnp.where(s == m, ids, big), axis=0, keepdims=True)
        hit = ids == i
        vals.append(m)
        sel.append(i)
        if payload is not None:
            pay.append(jnp.max(jnp.where(hit, payload, -1), axis=0, keepdims=True))
        s = jnp.where(hit, NEG, s)
    return vals, sel, pay


CODE_ROWS = 16


def _extract_top_unique(scores, code, side_work=None):
    state = list(scores)
    res = [([], []) for _ in state]
    for rnd in range(TOPK):
        for k, s in enumerate(state):
            m = jnp.max(s, axis=0, keepdims=True)
            hit = s == m
            res[k][0].append(m)
            res[k][1].append(jnp.dot(code, jnp.where(hit, 1.0, 0.0).astype(_bf16), preferred_element_type=_f32))
            state[k] = jnp.where(hit, NEG, s)
        if side_work is not None:
            side_work(rnd)
    return res


W_PITCH = N_KEYS + 8


def _peer_route_kernel(xn_ref, wq_ref, k1_ref, k2_ref, w_ref, ep_ref, gp_ref, wscr_ref, qs_ref, v1_ref, i1_ref,
                       v2_ref, i2_ref, tv_ref, pc_ref, qc_ref, et_ref, gt_ref):
    tile = xn_ref.shape[0]
    n_groups = N_HEADS // TOPK_CHAINS
    assert tile == TOPK_COLS and n_groups * 2 * TOPK * 2 == tile

    @pl.when(pl.program_id(0) == 0)
    def _():
        ep_ref[...] = jnp.zeros(ep_ref.shape, jnp.int32)
        gp_ref[...] = jnp.zeros(gp_ref.shape, _f32)

    sub = lax.broadcasted_iota(jnp.int32, (N_KEYS, N_SLOTS), 0)
    zero_blk = jnp.zeros((N_KEYS, N_SLOTS), _bf16)

    def build_pair(p):
        ga, hb = [], []
        for u in (2 * p, 2 * p + 1):
            e = ep_ref[u:u + 1, :]
            ga.append(jnp.where(sub == (e >> 7), gp_ref[u:u + 1, :], 0.0).astype(_bf16))
            hb.append(jnp.where(sub == (e & (N_KEYS - 1)), 1.0, 0.0).astype(_bf16))
        lhs = jnp.concatenate(ga, axis=1)
        rhs = jnp.concatenate([jnp.concatenate([hb[0], zero_blk], axis=1),
                               jnp.concatenate([zero_blk, hb[1]], axis=1)], axis=0)
        w2 = lax.dot_general(lhs, rhs, (((1,), (1,)), ((), ())), preferred_element_type=_f32)
        for k, u in enumerate((2 * p, 2 * p + 1)):
            wscr_ref[u * W_PITCH:u * W_PITCH + N_KEYS, :] = w2[:, k * N_KEYS:(k + 1) * N_KEYS]

    def group_queries(g):
        q = jnp.dot(xn_ref[...], wq_ref[g], preferred_element_type=_f32)
        return [q[:, c * 2 * PEER_HALF:(c + 1) * 2 * PEER_HALF].astype(_bf16) for c in range(TOPK_CHAINS)]

    def slot_rows(h):
        return pl.ds(h * TOPK if isinstance(h, int) else pl.multiple_of(h * TOPK, TOPK), TOPK)

    def project(g, slot):
        for c, qh in enumerate(group_queries(g)):
            qs_ref[slot, c] = qh
    key_id = lax.broadcasted_iota(jnp.int32, (N_KEYS, TOPK_COLS), 0)

    def cand_codes(row):
        low = row < TOPK
        mid = row < TOPK + 8 * 7
        cq = jnp.where(low, 0, jnp.where(mid, ((row - TOPK) >> 3) + 1, row - (TOPK + 8 * 7) + 8))
        cp = jnp.where(low, row, jnp.where(mid, (row - TOPK) & 7, 0))
        ok = ((cp + 1) * (cq + 1) <= TOPK) & (row < TOPK + 8 * 8)
        return cp, cq, ok

    n_rows = N_KEYS
    cp, cq, cand_ok = cand_codes(lax.broadcasted_iota(jnp.int32, (n_rows, TOPK_COLS), 0))
    cand_id = cp * TOPK + cq
    code_row = lax.broadcasted_iota(jnp.int32, (CODE_ROWS, n_rows), 0)
    code_col = lax.broadcasted_iota(jnp.int32, (CODE_ROWS, n_rows), 1)
    key_code = jnp.where(code_row == 0, code_col, jnp.where(code_row == 1, 1, 0)).astype(_f32).astype(_bf16)
    lp, lq, _ = cand_codes(code_col)
    pair_code = jnp.where(code_row == 0, lp, jnp.where(code_row == 1, lq, jnp.where(code_row == 2, 1, 0)))
    pair_code = pair_code.astype(_f32).astype(_bf16)

    def half_scores(h, qh, cols, half):
        kref = (k1_ref, k2_ref)[half]
        return lax.dot_general(kref[h], qh[cols, half * PEER_HALF:(half + 1) * PEER_HALF],
                               (((1,), (1,)), ((), ())), preferred_element_type=_f32)

    def candidates(c, cols):
        v1 = v1_ref[c, :, cols]
        i1 = i1_ref[c, :, cols] * N_KEYS
        cv = [v1 + v2_ref[c, 0:1, cols]]
        ce = [i1 + i2_ref[c, 0:1, cols]]
        for qq in range(1, 8):
            cv.append(v1[0:8] + v2_ref[c, qq:qq + 1, cols])
            ce.append(i1[0:8] + i2_ref[c, qq:qq + 1, cols])
        cv.append(v1[0:1] + v2_ref[c, 8:16, cols])
        ce.append(i1[0:1] + i2_ref[c, 8:16, cols])
        pad = n_rows - (TOPK + 8 * 8)
        cv.append(jnp.full((pad, TOPK_COLS), NEG, _f32))
        ce.append(jnp.zeros((pad, TOPK_COLS), jnp.int32))
        return jnp.where(cand_ok, jnp.concatenate(cv, axis=0), NEG), jnp.concatenate(ce, axis=0)

    def gates(c, h):
        tv = tv_ref[c]
        ex = jnp.exp(tv - tv[0:1, :])
        gt_ref[slot_rows(h), :] = ex / jnp.sum(ex, axis=0, keepdims=True)

    def head_group(i, slot, most_hits):
        heads = [(c, i * TOPK_CHAINS + c) for c in range(TOPK_CHAINS)]
        for lt in range(tile // TOPK_COLS):
            cols = slice(lt * TOPK_COLS, (lt + 1) * TOPK_COLS)
            halves = [(c, h, half) for c, h in heads for half in range(2)]
            found = _extract_top_unique([half_scores(h, qs_ref[slot, c], cols, half) for c, h, half in halves],
                                        key_code, lambda rnd: build_pair(i * 2 * TOPK + rnd))
            for (c, h, half), (vals, outs) in zip(halves, found):
                vref, iref = ((v1_ref, i1_ref), (v2_ref, i2_ref))[half]
                for kk in range(TOPK):
                    vref[c, kk:kk + 1, cols] = vals[kk]
                    iref[c, kk:kk + 1, cols] = outs[kk][0:1].astype(jnp.int32)
                    most_hits = jnp.maximum(most_hits, outs[kk][1:2])
            found = _extract_top_unique([candidates(c, cols)[0] for c, _ in heads], pair_code,
                                        lambda rnd: build_pair(i * 2 * TOPK + TOPK + rnd))
            for (c, h), (top, outs) in zip(heads, found):
                for kk in range(TOPK):
                    tv_ref[c, kk:kk + 1, cols] = top[kk]
                    pc_ref[c, kk:kk + 1, cols] = outs[kk][0:1]
                    qc_ref[c, kk:kk + 1, cols] = outs[kk][1:2]
                    most_hits = jnp.maximum(most_hits, outs[kk][2:3])
                pc = pc_ref[c, :, cols]
                qc = qc_ref[c, :, cols]
                e1 = jnp.zeros((TOPK, TOPK_COLS), jnp.int32)
                e2 = jnp.zeros((TOPK, TOPK_COLS), jnp.int32)
                for r in range(TOPK):
                    e1 = jnp.where(pc == float(r), i1_ref[c, r:r + 1, cols], e1)
                    e2 = jnp.where(qc == float(r), i2_ref[c, r:r + 1, cols], e2)
                et_ref[lt, slot_rows(h), :] = e1 * N_KEYS + e2
        for c, h in heads:
            gates(c, h)
        return most_hits

    def group_general(g, carry):
        for c, qh in enumerate(group_queries(g)):
            h = g * TOPK_CHAINS + c
            for lt in range(tile // TOPK_COLS):
                cols = slice(lt * TOPK_COLS, (lt + 1) * TOPK_COLS)
                for half, (vref, iref) in enumerate(((v1_ref, i1_ref), (v2_ref, i2_ref))):
                    vals, sel, _ = _extract_top(half_scores(h, qh, cols, half), key_id, N_KEYS)
                    for kk in range(TOPK):
                        vref[c, kk:kk + 1, cols] = vals[kk]
                        iref[c, kk:kk + 1, cols] = sel[kk]
                cv, ce = candidates(c, cols)
                top, _, picked = _extract_top(cv, cand_id, TOPK * TOPK, payload=ce)
                for kk in range(TOPK):
                    tv_ref[c, kk:kk + 1, cols] = top[kk]
                    et_ref[lt, pl.ds(h * TOPK + kk, 1), :] = picked[kk]
            gates(c, h)
        return carry

    most_hits = jnp.zeros((1, TOPK_COLS), _f32)
    project(0, 0)
    for g in range(n_groups):
        if g + 1 < n_groups:
            project(g + 1, (g + 1) % 2)
        most_hits = head_group(g, g % 2, most_hits)

    @pl.when(jnp.max(most_hits) > 1.5)
    def _():
        lax.fori_loop(0, n_groups, group_general, 0)

    def regroup(j, c):
        for t in range(4):
            r = j * 4 + t
            w_ref[r] = wscr_ref[pl.ds(r, tile, stride=W_PITCH), :].astype(_bf16)
        return c

    lax.fori_loop(0, N_KEYS // 4, regroup, 0)
    ep_ref[...] = et_ref[0].T
    gp_ref[...] = gt_ref[...].T


def _peer_route(xn, wq_g, k1, k2, tile):
    n, d = xn.shape
    nt = n // tile
    const = lambda a: pl.BlockSpec(a.shape, lambda i: (0,) * a.ndim, pipeline_mode=pl.Buffered(1))
    return pl.pallas_call(
        _peer_route_kernel,
        grid=(nt + 1,),
        in_specs=[pl.BlockSpec((tile, d), lambda i: (jnp.minimum(i, nt - 1), 0)), const(wq_g), const(k1), const(k2)],
        out_specs=pl.BlockSpec((N_KEYS, tile, N_KEYS), lambda i: (0, jnp.maximum(i - 1, 0), 0)),
        out_shape=jax.ShapeDtypeStruct((N_KEYS, n, N_KEYS), _bf16),
        scratch_shapes=[pltpu.VMEM((tile, N_SLOTS), jnp.int32), pltpu.VMEM((tile, N_SLOTS), _f32),
                        pltpu.VMEM((tile * W_PITCH, N_KEYS), _f32),
                        pltpu.VMEM((2, TOPK_CHAINS, tile, 2 * PEER_HALF), _bf16),
                        pltpu.VMEM((TOPK_CHAINS, TOPK, tile), _f32), pltpu.VMEM((TOPK_CHAINS, TOPK, tile), jnp.int32),
                        pltpu.VMEM((TOPK_CHAINS, TOPK, tile), _f32), pltpu.VMEM((TOPK_CHAINS, TOPK, tile), jnp.int32),
                        pltpu.VMEM((TOPK_CHAINS, TOPK, tile), _f32), pltpu.VMEM((TOPK_CHAINS, TOPK, tile), _f32),
                        pltpu.VMEM((TOPK_CHAINS, TOPK, tile), _f32),
                        pltpu.VMEM((tile // TOPK_COLS, N_SLOTS, TOPK_COLS), jnp.int32),
                        pltpu.VMEM((N_SLOTS, tile), _f32)],
        compiler_params=_cparams(("arbitrary",)),
        name="peer_route",
    )(xn, wq_g, k1, k2)


def _peer_ffn_kernel(nblk, xn_ref, x1_ref, ut_ref, v_ref, w_ref, o_ref, pa_ref, pb_ref):
    j = pl.program_id(1)
    rows = w_ref.shape[0]

    @pl.when(j == 0)
    def _():
        o_ref[...] = x1_ref[...]
        pb_ref[...] = jnp.zeros(pb_ref.shape, _bf16)

    def step(p_new_ref, p_prev_ref):
        if p_new_ref is not None:
            a = jnp.dot(xn_ref[...], ut_ref[...], preferred_element_type=_f32)
            w = jnp.concatenate([w_ref[r] for r in range(rows)], axis=-1).astype(_f32)
            p_new_ref[...] = (_gelu(a) * w).astype(_bf16)
        o_ref[...] += jnp.dot(p_prev_ref[...], v_ref[...], preferred_element_type=_f32)

    last = nblk
    assert nblk % 2 == 0

    @pl.when((lax.rem(j, 2) == 0) & (j < last))
    def _():
        step(pa_ref, pb_ref)

    @pl.when(lax.rem(j, 2) == 1)
    def _():
        step(pb_ref, pa_ref)

    @pl.when(j == last)
    def _():
        step(None, pb_ref)


def _peer_ffn(xn, x1, ut, v, wb, tile, eblk):
    n, d = xn.shape
    nblk = v.shape[0] // eblk
    rows = eblk // N_KEYS
    cur = lambda j: jnp.minimum(j, nblk - 1)
    prev = lambda j: jnp.maximum(j - 1, 0)
    return pl.pallas_call(
        functools.partial(_peer_ffn_kernel, nblk),
        grid=(n // tile, nblk + 1),
        in_specs=[
            pl.BlockSpec((tile, d), lambda i, j: (i, 0), pipeline_mode=pl.Buffered(1)),
            pl.BlockSpec((tile, d), lambda i, j: (i, 0), pipeline_mode=pl.Buffered(1)),
            pl.BlockSpec((d, eblk), lambda i, j: (0, cur(j))),
            pl.BlockSpec((eblk, d), lambda i, j: (prev(j), 0)),
            pl.BlockSpec((rows, tile, N_KEYS), lambda i, j: (cur(j), i, 0)),
        ],
        out_specs=pl.BlockSpec((tile, d), lambda i, j: (i, 0)),
        out_shape=jax.ShapeDtypeStruct((n, d), _f32),
        scratch_shapes=[pltpu.VMEM((tile, eblk), _bf16), pltpu.VMEM((tile, eblk), _bf16)],
        compiler_params=_cparams(("arbitrary", "arbitrary")),
        name="peer_ffn",
    )(xn, x1, ut, v, wb)


def _pad_rope_cols(w):
    lead = w.shape[:-1]
    w = w.reshape(*lead, N_HEADS, QK_DIM)
    w = jnp.pad(w, [(0, 0)] * len(lead) + [(0, 0), (0, QK_PAD - QK_DIM)])
    return w.reshape(*lead, N_HEADS * QK_PAD)


def _layer(x2, pos, invf, b, s, tiles, mix_g, w_in, q_g, w_uq, kv_g, w_ukv, qh_g, kh_g, conv_w, conv_b, w_rg,
           b_rg, w_ig, b_ig, lam, ao_g, ro_g, w_out, ffn_g, w_q, k1, k2, u_tab, v_tab):
    d = x2.shape[1]
    row = lambda a: a.reshape(1, -1)
    off_kr = Q_RANK + KV_RANK
    off_xr = off_kr + ROPE
    w_in_p = jnp.concatenate(
        [w_in[:, :off_xr], jnp.zeros((d, LANES - ROPE), w_in.dtype), w_in[:, off_xr:]], axis=1).astype(_bf16)
    cq, ckv, kr, xr, yg = _proj(x2, row(mix_g), w_in_p, tiles["proj"])

    pad_g = lambda g: jnp.pad(g, (0, QK_PAD - QK_DIM)).reshape(1, QK_PAD)
    q, k, v = _mla_prep(cq, ckv, kr, pos, invf, row(q_g), _pad_rope_cols(w_uq).astype(_bf16), row(kv_g),
                        w_ukv.astype(_bf16), pad_g(qh_g), pad_g(kh_g), b, s, tiles["prep"])
    attn = _attention(q, k, v, tiles["attn"]).reshape(b * s, ATTN_W)

    rec = _rglru(xr, yg, conv_w, row(conv_b), w_rg.astype(_bf16), row(b_rg), w_ig.astype(_bf16), row(b_ig),
                 row(lam), b, s, tiles["rec"])

    x1, xn = _out_proj(x2, attn, rec, row(ao_g), row(ro_g), w_out.astype(_bf16), row(ffn_g), tiles["out"])

    wq_g = w_q.reshape(d, N_HEADS // TOPK_CHAINS, -1).transpose(1, 0, 2).astype(_bf16)
    wb = _peer_route(xn, wq_g, k1.astype(_bf16), k2.astype(_bf16), tiles["topk"])
    return _peer_ffn(xn, x1, u_tab.astype(_bf16).T, v_tab.astype(_bf16), wb, tiles["ffn"], tiles["eblk"])


_TILES = dict(proj=256, prep=256, attn=2048, rec=256, out=256, topk=256, ffn=1024, eblk=512)


def _forward(tiles, x, positions, mix_norm_g, w_in, q_norm_g, w_uq, kv_norm_g, w_ukv, q_head_norm_g,
             k_head_norm_g, conv_w, conv_b, w_rgate, b_rgate, w_igate, b_igate, lru_lambda, attn_out_norm_g,
             rec_out_norm_g, w_out, ffn_norm_g, peer_w_q, peer_keys_1, peer_keys_2, peer_u, peer_v):
    b, s, d = x.shape
    half = ROPE // 2
    freq = ROPE_THETA ** (-jnp.arange(half, dtype=_f32) / half)
    invf = jnp.concatenate([freq, freq, jnp.zeros((LANES - ROPE,), _f32)]).reshape(1, LANES)
    pos = positions.reshape(b * s, 1)
    x2 = x.reshape(b * s, d)
    for l in range(mix_norm_g.shape[0]):
        x2 = _layer(x2, pos, invf, b, s, tiles, mix_norm_g[l], w_in[l], q_norm_g[l], w_uq[l], kv_norm_g[l],
                    w_ukv[l], q_head_norm_g[l], k_head_norm_g[l], conv_w[l], conv_b[l], w_rgate[l], b_rgate[l],
                    w_igate[l], b_igate[l], lru_lambda[l], attn_out_norm_g[l], rec_out_norm_g[l], w_out[l],
                    ffn_norm_g[l], peer_w_q[l], peer_keys_1[l], peer_keys_2[l], peer_u[l], peer_v[l])
    return x2.reshape(b, s, d)


def kernel(x, positions, mix_norm_g, w_in, q_norm_g, w_uq, kv_norm_g, w_ukv, q_head_norm_g, k_head_norm_g, conv_w, conv_b, w_rgate, b_rgate, w_igate, b_igate, lru_lambda, attn_out_norm_g, rec_out_norm_g, w_out, ffn_norm_g, peer_w_q, peer_keys_1, peer_keys_2, peer_u, peer_v):
    return _forward(_TILES, x, positions, mix_norm_g, w_in, q_norm_g, w_uq, kv_norm_g, w_ukv, q_head_norm_g,
                    k_head_norm_g, conv_w, conv_b, w_rgate, b_rgate, w_igate, b_igate, lru_lambda,
                    attn_out_norm_g, rec_out_norm_g, w_out, ffn_norm_g, peer_w_q, peer_keys_1, peer_keys_2,
                    peer_u, peer_v)
```

```python
import functools
import math

import jax
import jax.numpy as jnp
from jax import lax
from jax.experimental import pallas as pl
from jax.experimental.pallas import tpu as pltpu

EPS = 1e-6
LANES = 128
N_HEADS = 8
NOPE = 128
ROPE = 64
QK_DIM = NOPE + ROPE
QK_PAD = 256
V_DIM = 128
Q_RANK = 512
KV_RANK = 256
REC_W = 1024
ATTN_W = 1024
CONV_W = 4
LRU_C = 8.0
ROPE_THETA = 10000.0
TOPK = 16
N_KEYS = 128
PEER_HALF = 128
N_SLOTS = N_HEADS * TOPK
TOPK_COLS = 256
TOPK_CHAINS = 2
NEG = -1e30
VMEM_LIMIT = 56 * 1024 * 1024

_f32 = jnp.float32
_bf16 = jnp.bfloat16


def _cparams(sem):
    return pltpu.CompilerParams(dimension_semantics=sem, vmem_limit_bytes=VMEM_LIMIT)


def _rms(t, g):
    ms = jnp.mean(t * t, axis=-1, keepdims=True)
    return t * lax.rsqrt(ms + EPS) * g


def _gelu(t):
    return 0.5 * t * (1.0 + lax.erf(t * (1.0 / math.sqrt(2.0))))


def _proj_kernel(x_ref, g_ref, w_ref, cq_ref, ckv_ref, kr_ref, xr_ref, yg_ref):
    h = _rms(x_ref[...], g_ref[...]).astype(_bf16)

    def mm(lo, hi):
        return jnp.dot(h, w_ref[:, lo:hi], preferred_element_type=_f32)

    cq_ref[...] = mm(0, 512)
    ckv_ref[...] = mm(512, 768)
    kr_ref[...] = mm(768, 896)
    xr_ref[...] = mm(896, 1920)
    yg_ref[...] = mm(1920, 2944)


def _proj(x2, g, w_in_p, tile):
    n, d = x2.shape
    cols = w_in_p.shape[1]
    widths = (Q_RANK, KV_RANK, LANES, REC_W, REC_W)
    return pl.pallas_call(
        _proj_kernel,
        grid=(n // tile,),
        in_specs=[
            pl.BlockSpec((tile, d), lambda i: (i, 0)),
            pl.BlockSpec((1, d), lambda i: (0, 0)),
            pl.BlockSpec((d, cols), lambda i: (0, 0)),
        ],
        out_specs=[pl.BlockSpec((tile, w), lambda i: (i, 0)) for w in widths],
        out_shape=[jax.ShapeDtypeStruct((n, w), _f32) for w in widths],
        compiler_params=_cparams(("arbitrary",)),
        name="proj",
    )(x2, g, w_in_p)


def _mla_prep_kernel(cq_ref, ckv_ref, kr_ref, pos_ref, invf_ref, qg_ref, wuq_ref, kvg_ref, wukv_ref,
                     qhg_ref, khg_ref, q_ref, k_ref, v_ref):
    tile = cq_ref.shape[0]
    ang = pos_ref[...].astype(_f32) * invf_ref[...]
    cosv = jnp.cos(ang)
    sinv = jnp.sin(ang)
    lane = lax.broadcasted_iota(jnp.int32, (tile, LANES), 1)
    sin_signed = jnp.where(lane < ROPE // 2, -sinv, jnp.where(lane < ROPE, sinv, 0.0))

    def rope(t):
        swapped = jnp.where(lane < ROPE // 2, pltpu.roll(t, LANES - ROPE // 2, 1), pltpu.roll(t, ROPE // 2, 1))
        return t * cosv + swapped * sin_signed

    scale = QK_DIM ** -0.5 * math.log2(math.e)
    qf =jnp.dot(_rms(cq_ref[...], qg_ref[...]).astype(_bf16), wuq_ref[...], preferred_element_type=_f32)
    kvf = jnp.dot(_rms(ckv_ref[...], kvg_ref[...]).astype(_bf16), wukv_ref[...], preferred_element_type=_f32)
    qhg = qhg_ref[...]
    khg = khg_ref[...]
    kr = kr_ref[...]
    kr_ss = jnp.sum(kr * kr, axis=-1, keepdims=True)
    kr_base = rope(kr * khg[:, NOPE:])
    ones_col = jnp.where(lane == 0, 1.0, 0.0)
    for h in range(N_HEADS):
        qh = qf[:, h * QK_PAD:(h + 1) * QK_PAD]
        r = lax.rsqrt(jnp.sum(qh * qh, axis=-1, keepdims=True) * (1.0 / QK_DIM) + EPS) * scale
        qn = qh * r * qhg
        q_ref[0, h] = jnp.concatenate([qn[:, :NOPE], rope(qn[:, NOPE:])], axis=-1).astype(_bf16)
        kn = kvf[:, h * QK_PAD:h * QK_PAD + NOPE]
        rk = lax.rsqrt((jnp.sum(kn * kn, axis=-1, keepdims=True) + kr_ss) * (1.0 / QK_DIM) + EPS)
        k_ref[0, h] = jnp.concatenate([kn * rk * khg[:, :NOPE], kr_base * rk], axis=-1).astype(_bf16)
        v_ref[0, h] = jnp.concatenate([kvf[:, h * QK_PAD + NOPE:(h + 1) * QK_PAD], ones_col], axis=-1).astype(_bf16)


def _mla_prep(cq, ckv, kr, pos, invf, qg, wuq_p, kvg, wukv, qhg_p, khg_p, b, s, tile):
    nt = s // tile
    tok = lambda w: pl.BlockSpec((tile, w), lambda bi, ti: (bi * nt + ti, 0))
    full = lambda a: pl.BlockSpec(a.shape, lambda bi, ti: (0,) * a.ndim)
    head_out = lambda w: pl.BlockSpec((1, N_HEADS, tile, w), lambda bi, ti: (bi, 0, ti, 0))
    return pl.pallas_call(
        _mla_prep_kernel,
        grid=(b, nt),
        in_specs=[tok(Q_RANK), tok(KV_RANK), tok(LANES), tok(1), full(invf), full(qg), full(wuq_p), full(kvg),
                  full(wukv), full(qhg_p), full(khg_p)],
        out_specs=[head_out(QK_PAD), head_out(QK_PAD), head_out(V_PAD)],
        out_shape=[jax.ShapeDtypeStruct((b, N_HEADS, s, QK_PAD), _bf16),
                   jax.ShapeDtypeStruct((b, N_HEADS, s, QK_PAD), _bf16),
                   jax.ShapeDtypeStruct((b, N_HEADS, s, V_PAD), _bf16)],
        compiler_params=_cparams(("arbitrary", "arbitrary")),
        name="mla_prep",
    )(cq, ckv, kr, pos, invf, qg, wuq_p, kvg, wukv, qhg_p, khg_p)


ATT_ROWS = 256
ATT_CHUNK = 1024
V_PAD = 256


def _attn_kernel(q_ref, k_ref, v_ref, o_ref, m_ref, acc_ref, sa_ref, sb_ref):
    tq = q_ref.shape[2]
    assert tq == 2 * ATT_CHUNK
    nsub = tq // ATT_ROWS
    qi = pl.program_id(2)
    n0 = 2 * qi
    m_ref[...] = jnp.full(m_ref.shape, NEG, _f32)
    acc_ref[...] = jnp.zeros(acc_ref.shape, _f32)

    def chunk(t):
        return pl.ds(pl.multiple_of(t * ATT_CHUNK, ATT_CHUNK), ATT_CHUNK)

    def scores(r, t, s_ref):
        q = q_ref[0, 0, r * ATT_ROWS:(r + 1) * ATT_ROWS, :]
        s_ref[r] = lax.dot_general(q, k_ref[0, 0, chunk(t), :], (((1,), (1,)), ((), ())),
                                   preferred_element_type=_f32)

    def update(r, t, s_ref, c=None):
        s = s_ref[r]
        if c is not None and (c + 1) * ATT_CHUNK - 1 > r * ATT_ROWS:
            row = lax.broadcasted_iota(jnp.int32, s.shape, 0) + r * ATT_ROWS
            col = lax.broadcasted_iota(jnp.int32, s.shape, 1) + c * ATT_CHUNK
            s = jnp.where(col <= row, s, NEG)
        m_prev = m_ref[r]
        m_new = jnp.maximum(m_prev, jnp.max(s, axis=-1, keepdims=True))
        alpha = jnp.exp2(m_prev - m_new)
        p = jnp.exp2(s - m_new[:, :1]).astype(_bf16)
        acc_ref[r] = jnp.concatenate([alpha, alpha], axis=-1) * acc_ref[r] + jnp.dot(
            p, v_ref[0, 0, chunk(t), :], preferred_element_type=_f32)
        m_ref[r] = m_new

    every = range(nsub)
    upper = [r for r in every if (r + 1) * ATT_ROWS > ATT_CHUNK]
    for r in every:
        scores(r, 0, sa_ref)

    def body(i, c):
        t = 2 * i
        for r in every:
            scores(r, t + 1, sb_ref)
        for r in every:
            update(r, t, sa_ref)
        for r in every:
            scores(r, t + 2, sa_ref)
        for r in every:
            update(r, t + 1, sb_ref)
        return c

    lax.fori_loop(0, qi, body, 0)
    for r in upper:
        scores(r, n0 + 1, sb_ref)
    for r in every:
        update(r, n0, sa_ref, c=0)
    for r in upper:
        update(r, n0 + 1, sb_ref, c=1)
    for r in range(nsub):
        acc = acc_ref[r]
        o_ref[0, r * ATT_ROWS:(r + 1) * ATT_ROWS, :] = acc[:, :V_DIM] / acc[:, V_DIM:V_DIM + 1]


def _attention(q, k, v, tq):
    b, h, s, _ = q.shape
    return pl.pallas_call(
        _attn_kernel,
        grid=(b, h, s // tq),
        in_specs=[
            pl.BlockSpec((1, 1, tq, QK_PAD), lambda bi, hi, qi: (bi, hi, qi, 0)),
            pl.BlockSpec((1, 1, s, QK_PAD), lambda bi, hi, qi: (bi, hi, 0, 0)),
            pl.BlockSpec((1, 1, s, V_PAD), lambda bi, hi, qi: (bi, hi, 0, 0)),
        ],
        out_specs=pl.BlockSpec((1, tq, V_DIM), lambda bi, hi, qi: (bi, qi, hi)),
        out_shape=jax.ShapeDtypeStruct((b, s, h * V_DIM), _f32),
        scratch_shapes=[pltpu.VMEM((tq // ATT_ROWS, ATT_ROWS, LANES), _f32),
                        pltpu.VMEM((tq // ATT_ROWS, ATT_ROWS, V_PAD), _f32),
                        pltpu.VMEM((tq // ATT_ROWS, ATT_ROWS, ATT_CHUNK), _f32),
                        pltpu.VMEM((tq // ATT_ROWS, ATT_ROWS, ATT_CHUNK), _f32)],
        compiler_params=_cparams(("arbitrary", "arbitrary", "arbitrary")),
        name="attn",
    )(q, k, v)


def _rglru_kernel(xr_ref, yg_ref, cw_ref, cb_ref, wr_ref, br_ref, wi_ref, bi_ref, lam_ref, o_ref,
                  ext_ref, h_ref):
    tile = xr_ref.shape[0]
    ti = pl.program_id(1)

    @pl.when(ti == 0)
    def _():
        ext_ref[0:8, :] = jnp.zeros((8, REC_W), _f32)
        h_ref[...] = jnp.zeros(h_ref.shape, _f32)

    x = xr_ref[...]
    ext_ref[8:, :] = x
    xc = cb_ref[...] + cw_ref[CONV_W - 1:CONV_W, :] * x
    for d in range(1, CONV_W):
        xc = xc + cw_ref[CONV_W - 1 - d:CONV_W - d, :] * ext_ref[8 - d:8 - d + tile, :]
    ext_ref[0:8, :] = x[tile - 8:, :]

    xcb = xc.astype(_bf16)
    rs, is_ = [], []
    for h in range(N_HEADS):
        xh = xcb[:, h * LANES:(h + 1) * LANES]
        rs.append(jnp.dot(xh, wr_ref[h], preferred_element_type=_f32))
        is_.append(jnp.dot(xh, wi_ref[h], preferred_element_type=_f32))
    r = jax.nn.sigmoid(jnp.concatenate(rs, axis=-1) + br_ref[...])
    i = jax.nn.sigmoid(jnp.concatenate(is_, axis=-1) + bi_ref[...])
    nl = -lam_ref[...]
    softplus = jnp.maximum(nl, 0.0) + jnp.log1p(jnp.exp(-jnp.abs(nl)))
    a = jnp.exp(-LRU_C * r * softplus)
    y = 1.0 - a * a
    bb = jnp.where(y > 0.0, y * lax.rsqrt(y), 0.0) * i * xc

    row = lax.broadcasted_iota(jnp.int32, (tile, REC_W), 0)
    d = 1
    while d < tile:
        keep = row >= d
        a_sh = jnp.where(keep, pltpu.roll(a, d, 0), 1.0)
        b_sh = jnp.where(keep, pltpu.roll(bb, d, 0), 0.0)
        bb = a * b_sh + bb
        a = a * a_sh
        d *= 2
    hh = bb + a * h_ref[0:1, :]
    h_ref[...] = jnp.broadcast_to(hh[tile - 1:tile, :], h_ref.shape)
    o_ref[...] = _gelu(yg_ref[...]) * hh


def _rglru(xr, yg, cw, cb, wr, br, wi, bi, lam, b, s, tile):
    nt = s // tile
    tok = pl.BlockSpec((tile, REC_W), lambda bi_, ti: (bi_ * nt + ti, 0))
    full = lambda a: pl.BlockSpec(a.shape, lambda bi_, ti: (0,) * a.ndim)
    return pl.pallas_call(
        _rglru_kernel,
        grid=(b, nt),
        in_specs=[tok, tok, full(cw), full(cb), full(wr), full(br), full(wi), full(bi), full(lam)],
        out_specs=tok,
        out_shape=jax.ShapeDtypeStruct((b * s, REC_W), _f32),
        scratch_shapes=[pltpu.VMEM((tile + 8, REC_W), _f32), pltpu.VMEM((8, REC_W), _f32)],
        compiler_params=_cparams(("arbitrary", "arbitrary")),
        name="rglru",
    )(xr, yg, cw, cb, wr, br, wi, bi, lam)


def _out_proj_kernel(x_ref, at_ref, rc_ref, ag_ref, rg_ref, wo_ref, fg_ref, x1_ref, xn_ref):
    an = _rms(at_ref[...], ag_ref[...]).astype(_bf16)
    rn = _rms(rc_ref[...], rg_ref[...]).astype(_bf16)
    y = jnp.dot(an, wo_ref[0:ATTN_W, :], preferred_element_type=_f32)
    y = y + jnp.dot(rn, wo_ref[ATTN_W:, :], preferred_element_type=_f32)
    x1 = x_ref[...] + y
    x1_ref[...] = x1
    xn_ref[...] = _rms(x1, fg_ref[...]).astype(_bf16)


def _out_proj(x2, attn, rec, ag, rg, wo, fg, tile):
    n, d = x2.shape
    tok = lambda w: pl.BlockSpec((tile, w), lambda i: (i, 0))
    full = lambda a: pl.BlockSpec(a.shape, lambda i: (0,) * a.ndim)
    return pl.pallas_call(
        _out_proj_kernel,
        grid=(n // tile,),
        in_specs=[tok(d), tok(ATTN_W), tok(REC_W), full(ag), full(rg), full(wo), full(fg)],
        out_specs=[tok(d), tok(d)],
        out_shape=[jax.ShapeDtypeStruct((n, d), _f32), jax.ShapeDtypeStruct((n, d), _bf16)],
        compiler_params=_cparams(("arbitrary",)),
        name="out_proj",
    )(x2, attn, rec, ag, rg, wo, fg)


def _extract_top(s, ids, big, payload=None):
    vals, sel, pay = [], [], []
    for _ in range(TOPK):
        m = jnp.max(s, axis=0, keepdims=True)
        i = jnp.min(jnp.where(s == m, ids, big), axis=0, keepdims=True)
        hit = ids == i
        vals.append(m)
        sel.append(i)
        if payload is not None:
            pay.append(jnp.max(jnp.where(hit, payload, -1), axis=0, keepdims=True))
        s = jnp.where(hit, NEG, s)
    return vals, sel, pay


CODE_ROWS = 16


def _extract_top_unique(scores, code, side_work=None):
    state = list(scores)
    res = [([], []) for _ in state]
    for rnd in range(TOPK):
        for k, s in enumerate(state):
            m = jnp.max(s, axis=0, keepdims=True)
            hit = s == m
            res[k][0].append(m)
            res[k][1].append(jnp.dot(code, jnp.where(hit, 1.0, 0.0).astype(_bf16), preferred_element_type=_f32))
            state[k] = jnp.where(hit, NEG, s)
        if side_work is not None:
            side_work(rnd)
    return res


W_PITCH = N_KEYS + 8
W_REGROUP = 16


def _peer_route_kernel(xn_ref, wq_ref, k1_ref, k2_ref, w_ref, ep_ref, gp_ref, wscr_ref, qs_ref, v1_ref, i1_ref,
                       v2_ref, i2_ref, tv_ref, pc_ref, qc_ref, et_ref, gt_ref):
    tile = xn_ref.shape[0]
    n_groups = N_HEADS // TOPK_CHAINS
    assert tile == TOPK_COLS and n_groups * 2 * TOPK * 2 == tile

    @pl.when(pl.program_id(0) == 0)
    def _():
        ep_ref[...] = jnp.zeros(ep_ref.shape, jnp.int32)
        gp_ref[...] = jnp.zeros(gp_ref.shape, _f32)

    sub = lax.broadcasted_iota(jnp.int32, (N_KEYS, N_SLOTS), 0)
    zero_blk = jnp.zeros((N_KEYS, N_SLOTS), _bf16)

    def build_pair(p):
        ga, hb = [], []
        for u in (2 * p, 2 * p + 1):
            e = ep_ref[u:u + 1, :]
            ga.append(jnp.where(sub == (e >> 7), gp_ref[u:u + 1, :], 0.0).astype(_bf16))
            hb.append(jnp.where(sub == (e & (N_KEYS - 1)), 1.0, 0.0).astype(_bf16))
        lhs = jnp.concatenate(ga, axis=1)
        rhs = jnp.concatenate([jnp.concatenate([hb[0], zero_blk], axis=1),
                               jnp.concatenate([zero_blk, hb[1]], axis=1)], axis=0)
        w2 = lax.dot_general(lhs, rhs, (((1,), (1,)), ((), ())), preferred_element_type=_f32)
        for k, u in enumerate((2 * p, 2 * p + 1)):
            wscr_ref[u * W_PITCH:u * W_PITCH + N_KEYS, :] = w2[:, k * N_KEYS:(k + 1) * N_KEYS]

    def group_queries(g):
        q = jnp.dot(xn_ref[...], wq_ref[g], preferred_element_type=_f32)
        return [q[:, c * 2 * PEER_HALF:(c + 1) * 2 * PEER_HALF].astype(_bf16) for c in range(TOPK_CHAINS)]

    def slot_rows(h):
        return pl.ds(h * TOPK if isinstance(h, int) else pl.multiple_of(h * TOPK, TOPK), TOPK)

    def project(g, slot):
        for c, qh in enumerate(group_queries(g)):
            qs_ref[slot, c] = qh
    key_id = lax.broadcasted_iota(jnp.int32, (N_KEYS, TOPK_COLS), 0)

    def cand_codes(row):
        low = row < TOPK
        mid = row < TOPK + 8 * 7
        cq = jnp.where(low, 0, jnp.where(mid, ((row - TOPK) >> 3) + 1, row - (TOPK + 8 * 7) + 8))
        cp = jnp.where(low, row, jnp.where(mid, (row - TOPK) & 7, 0))
        ok = ((cp + 1) * (cq + 1) <= TOPK) & (row < TOPK + 8 * 8)
        return cp, cq, ok

    n_rows = N_KEYS
    cp, cq, cand_ok = cand_codes(lax.broadcasted_iota(jnp.int32, (n_rows, TOPK_COLS), 0))
    cand_id = cp * TOPK + cq
    code_row = lax.broadcasted_iota(jnp.int32, (CODE_ROWS, n_rows), 0)
    code_col = lax.broadcasted_iota(jnp.int32, (CODE_ROWS, n_rows), 1)
    key_code = jnp.where(code_row == 0, code_col, jnp.where(code_row == 1, 1, 0)).astype(_f32).astype(_bf16)
    lp, lq, _ = cand_codes(code_col)
    pair_code = jnp.where(code_row == 0, lp, jnp.where(code_row == 1, lq, jnp.where(code_row == 2, 1, 0)))
    pair_code = pair_code.astype(_f32).astype(_bf16)

    def half_scores(h, qh, cols, half):
        kref = (k1_ref, k2_ref)[half]
        return lax.dot_general(kref[h], qh[cols, half * PEER_HALF:(half + 1) * PEER_HALF],
                               (((1,), (1,)), ((), ())), preferred_element_type=_f32)

    def candidates(c, cols):
        v1 = v1_ref[c, :, cols]
        i1 = i1_ref[c, :, cols] * N_KEYS
        cv = [v1 + v2_ref[c, 0:1, cols]]
        ce = [i1 + i2_ref[c, 0:1, cols]]
        for qq in range(1, 8):
            cv.append(v1[0:8] + v2_ref[c, qq:qq + 1, cols])
            ce.append(i1[0:8] + i2_ref[c, qq:qq + 1, cols])
        cv.append(v1[0:1] + v2_ref[c, 8:16, cols])
        ce.append(i1[0:1] + i2_ref[c, 8:16, cols])
        pad = n_rows - (TOPK + 8 * 8)
        cv.append(jnp.full((pad, TOPK_COLS), NEG, _f32))
        ce.append(jnp.zeros((pad, TOPK_COLS), jnp.int32))
        return jnp.where(cand_ok, jnp.concatenate(cv, axis=0), NEG), jnp.concatenate(ce, axis=0)

    def gates(c, h):
        tv = tv_ref[c]
        ex = jnp.exp(tv - tv[0:1, :])
        gt_ref[slot_rows(h), :] = ex / jnp.sum(ex, axis=0, keepdims=True)

    def head_group(i, slot, most_hits):
        heads = [(c, i * TOPK_CHAINS + c) for c in range(TOPK_CHAINS)]
        for lt in range(tile // TOPK_COLS):
            cols = slice(lt * TOPK_COLS, (lt + 1) * TOPK_COLS)
            halves = [(c, h, half) for c, h in heads for half in range(2)]
            found = _extract_top_unique([half_scores(h, qs_ref[slot, c], cols, half) for c, h, half in halves],
                                        key_code, lambda rnd: build_pair(i * 2 * TOPK + rnd))
            for (c, h, half), (vals, outs) in zip(halves, found):
                vref, iref = ((v1_ref, i1_ref), (v2_ref, i2_ref))[half]
                for kk in range(TOPK):
                    vref[c, kk:kk + 1, cols] = vals[kk]
                    iref[c, kk:kk + 1, cols] = outs[kk][0:1].astype(jnp.int32)
                    most_hits = jnp.maximum(most_hits, outs[kk][1:2])
            found = _extract_top_unique([candidates(c, cols)[0] for c, _ in heads], pair_code,
                                        lambda rnd: build_pair(i * 2 * TOPK + TOPK + rnd))
            for (c, h), (top, outs) in zip(heads, found):
                for kk in range(TOPK):
                    tv_ref[c, kk:kk + 1, cols] = top[kk]
                    pc_ref[c, kk:kk + 1, cols] = outs[kk][0:1]
                    qc_ref[c, kk:kk + 1, cols] = outs[kk][1:2]
                    most_hits = jnp.maximum(most_hits, outs[kk][2:3])
                pc = pc_ref[c, :, cols]
                qc = qc_ref[c, :, cols]
                e1 = jnp.zeros((TOPK, TOPK_COLS), jnp.int32)
                e2 = jnp.zeros((TOPK, TOPK_COLS), jnp.int32)
                for r in range(TOPK):
                    e1 = jnp.where(pc == float(r), i1_ref[c, r:r + 1, cols], e1)
                    e2 = jnp.where(qc == float(r), i2_ref[c, r:r + 1, cols], e2)
                et_ref[lt, slot_rows(h), :] = e1 * N_KEYS + e2
        for c, h in heads:
            gates(c, h)
        return most_hits

    def group_general(g, carry):
        for c, qh in enumerate(group_queries(g)):
            h = g * TOPK_CHAINS + c
            for lt in range(tile // TOPK_COLS):
                cols = slice(lt * TOPK_COLS, (lt + 1) * TOPK_COLS)
                for half, (vref, iref) in enumerate(((v1_ref, i1_ref), (v2_ref, i2_ref))):
                    vals, sel, _ = _extract_top(half_scores(h, qh, cols, half), key_id, N_KEYS)
                    for kk in range(TOPK):
                        vref[c, kk:kk + 1, cols] = vals[kk]
                        iref[c, kk:kk + 1, cols] = sel[kk]
                cv, ce = candidates(c, cols)
                top, _, picked = _extract_top(cv, cand_id, TOPK * TOPK, payload=ce)
                for kk in range(TOPK):
                    tv_ref[c, kk:kk + 1, cols] = top[kk]
                    et_ref[lt, pl.ds(h * TOPK + kk, 1), :] = picked[kk]
            gates(c, h)
        return carry

    most_hits = jnp.zeros((1, TOPK_COLS), _f32)
    project(0, 0)
    for g in range(n_groups):
        if g + 1 < n_groups:
            project(g + 1, (g + 1) % 2)
        most_hits = head_group(g, g % 2, most_hits)

    @pl.when(jnp.max(most_hits) > 1.5)
    def _():
        lax.fori_loop(0, n_groups, group_general, 0)

    def regroup(j, c):
        for t in range(W_REGROUP):
            r = j * W_REGROUP + t
            w_ref[r] = wscr_ref[pl.ds(r, tile, stride=W_PITCH), :].astype(_bf16)
        return c

    lax.fori_loop(0, N_KEYS // W_REGROUP, regroup, 0)
    ep_ref[...] = et_ref[0].T
    gp_ref[...] = gt_ref[...].T


def _peer_route(xn, wq_g, k1, k2, tile):
    n, d = xn.shape
    nt = n // tile
    const = lambda a: pl.BlockSpec(a.shape, lambda i: (0,) * a.ndim, pipeline_mode=pl.Buffered(1))
    return pl.pallas_call(
        _peer_route_kernel,
        grid=(nt + 1,),
        in_specs=[pl.BlockSpec((tile, d), lambda i: (jnp.minimum(i, nt - 1), 0)), const(wq_g), const(k1), const(k2)],
        out_specs=pl.BlockSpec((N_KEYS, tile, N_KEYS), lambda i: (0, jnp.maximum(i - 1, 0), 0)),
        out_shape=jax.ShapeDtypeStruct((N_KEYS, n, N_KEYS), _bf16),
        scratch_shapes=[pltpu.VMEM((tile, N_SLOTS), jnp.int32), pltpu.VMEM((tile, N_SLOTS), _f32),
                        pltpu.VMEM((tile * W_PITCH, N_KEYS), _f32),
                        pltpu.VMEM((2, TOPK_CHAINS, tile, 2 * PEER_HALF), _bf16),
                        pltpu.VMEM((TOPK_CHAINS, TOPK, tile), _f32), pltpu.VMEM((TOPK_CHAINS, TOPK, tile), jnp.int32),
                        pltpu.VMEM((TOPK_CHAINS, TOPK, tile), _f32), pltpu.VMEM((TOPK_CHAINS, TOPK, tile), jnp.int32),
                        pltpu.VMEM((TOPK_CHAINS, TOPK, tile), _f32), pltpu.VMEM((TOPK_CHAINS, TOPK, tile), _f32),
                        pltpu.VMEM((TOPK_CHAINS, TOPK, tile), _f32),
                        pltpu.VMEM((tile // TOPK_COLS, N_SLOTS, TOPK_COLS), jnp.int32),
                        pltpu.VMEM((N_SLOTS, tile), _f32)],
        compiler_params=_cparams(("arbitrary",)),
        name="peer_route",
    )(xn, wq_g, k1, k2)


def _peer_ffn_kernel(nblk, xn_ref, x1_ref, ut_ref, v_ref, w_ref, o_ref, pa_ref, pb_ref):
    j = pl.program_id(1)
    rows = w_ref.shape[0]

    @pl.when(j == 0)
    def _():
        o_ref[...] = x1_ref[...]
        pb_ref[...] = jnp.zeros(pb_ref.shape, _bf16)

    def step(p_new_ref, p_prev_ref):
        if p_new_ref is not None:
            a = jnp.dot(xn_ref[...], ut_ref[...], preferred_element_type=_f32)
            w = jnp.concatenate([w_ref[r] for r in range(rows)], axis=-1).astype(_f32)
            p_new_ref[...] = (_gelu(a) * w).astype(_bf16)
        o_ref[...] += jnp.dot(p_prev_ref[...], v_ref[...], preferred_element_type=_f32)

    last = nblk
    assert nblk % 2 == 0

    @pl.when((lax.rem(j, 2) == 0) & (j < last))
    def _():
        step(pa_ref, pb_ref)

    @pl.when(lax.rem(j, 2) == 1)
    def _():
        step(pb_ref, pa_ref)

    @pl.when(j == last)
    def _():
        step(None, pb_ref)


def _peer_ffn(xn, x1, ut, v, wb, tile, eblk):
    n, d = xn.shape
    nblk = v.shape[0] // eblk
    rows = eblk // N_KEYS
    cur = lambda j: jnp.minimum(j, nblk - 1)
    prev = lambda j: jnp.maximum(j - 1, 0)
    return pl.pallas_call(
        functools.partial(_peer_ffn_kernel, nblk),
        grid=(n // tile, nblk + 1),
        in_specs=[
            pl.BlockSpec((tile, d), lambda i, j: (i, 0), pipeline_mode=pl.Buffered(1)),
            pl.BlockSpec((tile, d), lambda i, j: (i, 0), pipeline_mode=pl.Buffered(1)),
            pl.BlockSpec((d, eblk), lambda i, j: (0, cur(j))),
            pl.BlockSpec((eblk, d), lambda i, j: (prev(j), 0)),
            pl.BlockSpec((rows, tile, N_KEYS), lambda i, j: (cur(j), i, 0)),
        ],
        out_specs=pl.BlockSpec((tile, d), lambda i, j: (i, 0)),
        out_shape=jax.ShapeDtypeStruct((n, d), _f32),
        scratch_shapes=[pltpu.VMEM((tile, eblk), _bf16), pltpu.VMEM((tile, eblk), _bf16)],
        compiler_params=_cparams(("arbitrary", "arbitrary")),
        name="peer_ffn",
    )(xn, x1, ut, v, wb)


def _pad_rope_cols(w):
    lead = w.shape[:-1]
    w = w.reshape(*lead, N_HEADS, QK_DIM)
    w = jnp.pad(w, [(0, 0)] * len(lead) + [(0, 0), (0, QK_PAD - QK_DIM)])
    return w.reshape(*lead, N_HEADS * QK_PAD)


def _layer(x2, pos, invf, b, s, tiles, mix_g, w_in, q_g, w_uq, kv_g, w_ukv, qh_g, kh_g, conv_w, conv_b, w_rg,
           b_rg, w_ig, b_ig, lam, ao_g, ro_g, w_out, ffn_g, w_q, k1, k2, u_tab, v_tab):
    d = x2.shape[1]
    row = lambda a: a.reshape(1, -1)
    off_kr = Q_RANK + KV_RANK
    off_xr = off_kr + ROPE
    w_in_p = jnp.concatenate(
        [w_in[:, :off_xr], jnp.zeros((d, LANES - ROPE), w_in.dtype), w_in[:, off_xr:]], axis=1).astype(_bf16)
    cq, ckv, kr, xr, yg = _proj(x2, row(mix_g), w_in_p, tiles["proj"])

    pad_g = lambda g: jnp.pad(g, (0, QK_PAD - QK_DIM)).reshape(1, QK_PAD)
    q, k, v = _mla_prep(cq, ckv, kr, pos, invf, row(q_g), _pad_rope_cols(w_uq).astype(_bf16), row(kv_g),
                        w_ukv.astype(_bf16), pad_g(qh_g), pad_g(kh_g), b, s, tiles["prep"])
    attn = _attention(q, k, v, tiles["attn"]).reshape(b * s, ATTN_W)

    rec = _rglru(xr, yg, conv_w, row(conv_b), w_rg.astype(_bf16), row(b_rg), w_ig.astype(_bf16), row(b_ig),
                 row(lam), b, s, tiles["rec"])

    x1, xn = _out_proj(x2, attn, rec, row(ao_g), row(ro_g), w_out.astype(_bf16), row(ffn_g), tiles["out"])

    wq_g = w_q.reshape(d, N_HEADS // TOPK_CHAINS, -1).transpose(1, 0, 2).astype(_bf16)
    wb = _peer_route(xn, wq_g, k1.astype(_bf16), k2.astype(_bf16), tiles["topk"])
    return _peer_ffn(xn, x1, u_tab.astype(_bf16).T, v_tab.astype(_bf16), wb, tiles["ffn"], tiles["eblk"])


_TILES = dict(proj=512, prep=256, attn=2048, rec=256, out=512, topk=256, ffn=1024, eblk=512)


def _forward(tiles, x, positions, mix_norm_g, w_in, q_norm_g, w_uq, kv_norm_g, w_ukv, q_head_norm_g,
             k_head_norm_g, conv_w, conv_b, w_rgate, b_rgate, w_igate, b_igate, lru_lambda, attn_out_norm_g,
             rec_out_norm_g, w_out, ffn_norm_g, peer_w_q, peer_keys_1, peer_keys_2, peer_u, peer_v):
    b, s, d = x.shape
    half = ROPE // 2
    freq = ROPE_THETA ** (-jnp.arange(half, dtype=_f32) / half)
    invf = jnp.concatenate([freq, freq, jnp.zeros((LANES - ROPE,), _f32)]).reshape(1, LANES)
    pos = positions.reshape(b * s, 1)
    x2 = x.reshape(b * s, d)
    for l in range(mix_norm_g.shape[0]):
        x2 = _layer(x2, pos, invf, b, s, tiles, mix_norm_g[l], w_in[l], q_norm_g[l], w_uq[l], kv_norm_g[l],
                    w_ukv[l], q_head_norm_g[l], k_head_norm_g[l], conv_w[l], conv_b[l], w_rgate[l], b_rgate[l],
                    w_igate[l], b_igate[l], lru_lambda[l], attn_out_norm_g[l], rec_out_norm_g[l], w_out[l],
                    ffn_norm_g[l], peer_w_q[l], peer_keys_1[l], peer_keys_2[l], peer_u[l], peer_v[l])
    return x2.reshape(b, s, d)


def kernel(x, positions, mix_norm_g, w_in, q_norm_g, w_uq, kv_norm_g, w_ukv, q_head_norm_g, k_head_norm_g, conv_w, conv_b, w_rgate, b_rgate, w_igate, b_igate, lru_lambda, attn_out_norm_g, rec_out_norm_g, w_out, ffn_norm_g, peer_w_q, peer_keys_1, peer_keys_2, peer_u, peer_v):
    return _forward(_TILES, x, positions, mix_norm_g, w_in, q_norm_g, w_uq, kv_norm_g, w_ukv, q_head_norm_g,
                    k_head_norm_g, conv_w, conv_b, w_rgate, b_rgate, w_igate, b_igate, lru_lambda,
                    attn_out_norm_g, rec_out_norm_g, w_out, ffn_norm_g, peer_w_q, peer_keys_1, peer_keys_2,
                    peer_u, peer_v)
```

```python
import functools
import math

import jax
import jax.numpy as jnp
from jax import lax
from jax.experimental import pallas as pl
from jax.experimental.pallas import tpu as pltpu

EPS = 1e-6
LANES = 128
N_HEADS = 8
NOPE = 128
ROPE = 64
QK_DIM = NOPE + ROPE
QK_PAD = 256
V_DIM = 128
Q_RANK = 512
KV_RANK = 256
REC_W = 1024
ATTN_W = 1024
CONV_W = 4
LRU_C = 8.0
ROPE_THETA = 10000.0
TOPK = 16
N_KEYS = 128
PEER_HALF = 128
N_SLOTS = N_HEADS * TOPK
TOPK_COLS = 256
TOPK_CHAINS = 2
NEG = -1e30
VMEM_LIMIT = 56 * 1024 * 1024

_f32 = jnp.float32
_bf16 = jnp.bfloat16


def _cparams(sem):
    return pltpu.CompilerParams(dimension_semantics=sem, vmem_limit_bytes=VMEM_LIMIT)


def _rms(t, g):
    ms = jnp.mean(t * t, axis=-1, keepdims=True)
    return t * lax.rsqrt(ms + EPS) * g


def _gelu(t):
    return 0.5 * t * (1.0 + lax.erf(t * (1.0 / math.sqrt(2.0))))


def _proj_kernel(x_ref, g_ref, w_ref, cq_ref, ckv_ref, kr_ref, xr_ref, yg_ref):
    h = _rms(x_ref[...], g_ref[...]).astype(_bf16)

    def mm(lo, hi):
        return jnp.dot(h, w_ref[:, lo:hi], preferred_element_type=_f32)

    cq_ref[...] = mm(0, 512)
    ckv_ref[...] = mm(512, 768)
    kr_ref[...] = mm(768, 896)
    xr_ref[...] = mm(896, 1920)
    yg_ref[...] = mm(1920, 2944)


def _proj(x2, g, w_in_p, tile):
    n, d = x2.shape
    cols = w_in_p.shape[1]
    widths = (Q_RANK, KV_RANK, LANES, REC_W, REC_W)
    return pl.pallas_call(
        _proj_kernel,
        grid=(n // tile,),
        in_specs=[
            pl.BlockSpec((tile, d), lambda i: (i, 0)),
            pl.BlockSpec((1, d), lambda i: (0, 0)),
            pl.BlockSpec((d, cols), lambda i: (0, 0)),
        ],
        out_specs=[pl.BlockSpec((tile, w), lambda i: (i, 0)) for w in widths],
        out_shape=[jax.ShapeDtypeStruct((n, w), _f32) for w in widths],
        compiler_params=_cparams(("arbitrary",)),
        name="proj",
    )(x2, g, w_in_p)


def _mla_prep_kernel(cq_ref, ckv_ref, kr_ref, pos_ref, invf_ref, qg_ref, wuq_ref, kvg_ref, wukv_ref,
                     qhg_ref, khg_ref, q_ref, k_ref, v_ref):
    tile = cq_ref.shape[0]
    ang = pos_ref[...].astype(_f32) * invf_ref[...]
    cosv = jnp.cos(ang)
    sinv = jnp.sin(ang)
    lane = lax.broadcasted_iota(jnp.int32, (tile, LANES), 1)
    sin_signed = jnp.where(lane < ROPE // 2, -sinv, jnp.where(lane < ROPE, sinv, 0.0))

    def rope(t):
        swapped = jnp.where(lane < ROPE // 2, pltpu.roll(t, LANES - ROPE // 2, 1), pltpu.roll(t, ROPE // 2, 1))
        return t * cosv + swapped * sin_signed

    scale = QK_DIM ** -0.5 * math.log2(math.e)
    qf =jnp.dot(_rms(cq_ref[...], qg_ref[...]).astype(_bf16), wuq_ref[...], preferred_element_type=_f32)
    kvf = jnp.dot(_rms(ckv_ref[...], kvg_ref[...]).astype(_bf16), wukv_ref[...], preferred_element_type=_f32)
    qhg = qhg_ref[...]
    khg = khg_ref[...]
    kr = kr_ref[...]
    kr_ss = jnp.sum(kr * kr, axis=-1, keepdims=True)
    kr_base = rope(kr * khg[:, NOPE:])
    ones_col = jnp.where(lane == 0, 1.0, 0.0)
    for h in range(N_HEADS):
        qh = qf[:, h * QK_PAD:(h + 1) * QK_PAD]
        r = lax.rsqrt(jnp.sum(qh * qh, axis=-1, keepdims=True) * (1.0 / QK_DIM) + EPS) * scale
        qn = qh * r * qhg
        q_ref[0, h] = jnp.concatenate([qn[:, :NOPE], rope(qn[:, NOPE:])], axis=-1).astype(_bf16)
        kn = kvf[:, h * QK_PAD:h * QK_PAD + NOPE]
        rk = lax.rsqrt((jnp.sum(kn * kn, axis=-1, keepdims=True) + kr_ss) * (1.0 / QK_DIM) + EPS)
        k_ref[0, h] = jnp.concatenate([kn * rk * khg[:, :NOPE], kr_base * rk], axis=-1).astype(_bf16)
        v_ref[0, h] = jnp.concatenate([kvf[:, h * QK_PAD + NOPE:(h + 1) * QK_PAD], ones_col], axis=-1).astype(_bf16)


def _mla_prep(cq, ckv, kr, pos, invf, qg, wuq_p, kvg, wukv, qhg_p, khg_p, b, s, tile):
    nt = s // tile
    tok = lambda w: pl.BlockSpec((tile, w), lambda bi, ti: (bi * nt + ti, 0))
    full = lambda a: pl.BlockSpec(a.shape, lambda bi, ti: (0,) * a.ndim)
    head_out = lambda w: pl.BlockSpec((1, N_HEADS, tile, w), lambda bi, ti: (bi, 0, ti, 0))
    return pl.pallas_call(
        _mla_prep_kernel,
        grid=(b, nt),
        in_specs=[tok(Q_RANK), tok(KV_RANK), tok(LANES), tok(1), full(invf), full(qg), full(wuq_p), full(kvg),
                  full(wukv), full(qhg_p), full(khg_p)],
        out_specs=[head_out(QK_PAD), head_out(QK_PAD), head_out(V_PAD)],
        out_shape=[jax.ShapeDtypeStruct((b, N_HEADS, s, QK_PAD), _bf16),
                   jax.ShapeDtypeStruct((b, N_HEADS, s, QK_PAD), _bf16),
                   jax.ShapeDtypeStruct((b, N_HEADS, s, V_PAD), _bf16)],
        compiler_params=_cparams(("arbitrary", "arbitrary")),
        name="mla_prep",
    )(cq, ckv, kr, pos, invf, qg, wuq_p, kvg, wukv, qhg_p, khg_p)


ATT_ROWS = 256
ATT_CHUNK = 1024
V_PAD = 256


def _attn_kernel(q_ref, k_ref, v_ref, o_ref, m_ref, acc_ref, sa_ref, sb_ref):
    tq = q_ref.shape[2]
    assert tq == 2 * ATT_CHUNK
    nsub = tq // ATT_ROWS
    qi = pl.program_id(2)
    n0 = 2 * qi
    m_ref[...] = jnp.full(m_ref.shape, NEG, _f32)
    acc_ref[...] = jnp.zeros(acc_ref.shape, _f32)

    def chunk(t):
        return pl.ds(pl.multiple_of(t * ATT_CHUNK, ATT_CHUNK), ATT_CHUNK)

    def scores(r, t, s_ref):
        q = q_ref[0, 0, r * ATT_ROWS:(r + 1) * ATT_ROWS, :]
        s_ref[r] = lax.dot_general(q, k_ref[0, 0, chunk(t), :], (((1,), (1,)), ((), ())),
                                   preferred_element_type=_f32)

    def update(r, t, s_ref, c=None):
        s = s_ref[r]
        if c is not None and (c + 1) * ATT_CHUNK - 1 > r * ATT_ROWS:
            row = lax.broadcasted_iota(jnp.int32, s.shape, 0) + r * ATT_ROWS
            col = lax.broadcasted_iota(jnp.int32, s.shape, 1) + c * ATT_CHUNK
            s = jnp.where(col <= row, s, NEG)
        m_prev = m_ref[r]
        m_new = jnp.maximum(m_prev, jnp.max(s, axis=-1, keepdims=True))
        alpha = jnp.exp2(m_prev - m_new)
        p = jnp.exp2(s - m_new[:, :1]).astype(_bf16)
        acc_ref[r] = jnp.concatenate([alpha, alpha], axis=-1) * acc_ref[r] + jnp.dot(
            p, v_ref[0, 0, chunk(t), :], preferred_element_type=_f32)
        m_ref[r] = m_new

    every = range(nsub)
    upper = [r for r in every if (r + 1) * ATT_ROWS > ATT_CHUNK]
    for r in every:
        scores(r, 0, sa_ref)

    def body(i, c):
        t = 2 * i
        for r in every:
            scores(r, t + 1, sb_ref)
        for r in every:
            update(r, t, sa_ref)
        for r in every:
            scores(r, t + 2, sa_ref)
        for r in every:
            update(r, t + 1, sb_ref)
        return c

    lax.fori_loop(0, qi, body, 0)
    for r in upper:
        scores(r, n0 + 1, sb_ref)
    for r in every:
        update(r, n0, sa_ref, c=0)
    for r in upper:
        update(r, n0 + 1, sb_ref, c=1)
    for r in range(nsub):
        acc = acc_ref[r]
        o_ref[0, r * ATT_ROWS:(r + 1) * ATT_ROWS, :] = acc[:, :V_DIM] / acc[:, V_DIM:V_DIM + 1]


def _attention(q, k, v, tq):
    b, h, s, _ = q.shape
    return pl.pallas_call(
        _attn_kernel,
        grid=(b, h, s // tq),
        in_specs=[
            pl.BlockSpec((1, 1, tq, QK_PAD), lambda bi, hi, qi: (bi, hi, qi, 0)),
            pl.BlockSpec((1, 1, s, QK_PAD), lambda bi, hi, qi: (bi, hi, 0, 0)),
            pl.BlockSpec((1, 1, s, V_PAD), lambda bi, hi, qi: (bi, hi, 0, 0)),
        ],
        out_specs=pl.BlockSpec((1, tq, V_DIM), lambda bi, hi, qi: (bi, qi, hi)),
        out_shape=jax.ShapeDtypeStruct((b, s, h * V_DIM), _f32),
        scratch_shapes=[pltpu.VMEM((tq // ATT_ROWS, ATT_ROWS, LANES), _f32),
                        pltpu.VMEM((tq // ATT_ROWS, ATT_ROWS, V_PAD), _f32),
                        pltpu.VMEM((tq // ATT_ROWS, ATT_ROWS, ATT_CHUNK), _f32),
                        pltpu.VMEM((tq // ATT_ROWS, ATT_ROWS, ATT_CHUNK), _f32)],
        compiler_params=_cparams(("arbitrary", "arbitrary", "arbitrary")),
        name="attn",
    )(q, k, v)


def _rglru_kernel(xr_ref, yg_ref, cw_ref, cb_ref, wr_ref, br_ref, wi_ref, bi_ref, lam_ref, o_ref,
                  ext_ref, h_ref):
    tile = xr_ref.shape[0]
    ti = pl.program_id(1)

    @pl.when(ti == 0)
    def _():
        ext_ref[0:8, :] = jnp.zeros((8, REC_W), _f32)
        h_ref[...] = jnp.zeros(h_ref.shape, _f32)

    x = xr_ref[...]
    ext_ref[8:, :] = x
    xc = cb_ref[...] + cw_ref[CONV_W - 1:CONV_W, :] * x
    for d in range(1, CONV_W):
        xc = xc + cw_ref[CONV_W - 1 - d:CONV_W - d, :] * ext_ref[8 - d:8 - d + tile, :]
    ext_ref[0:8, :] = x[tile - 8:, :]

    xcb = xc.astype(_bf16)
    rs, is_ = [], []
    for h in range(N_HEADS):
        xh = xcb[:, h * LANES:(h + 1) * LANES]
        rs.append(jnp.dot(xh, wr_ref[h], preferred_element_type=_f32))
        is_.append(jnp.dot(xh, wi_ref[h], preferred_element_type=_f32))
    r = jax.nn.sigmoid(jnp.concatenate(rs, axis=-1) + br_ref[...])
    i = jax.nn.sigmoid(jnp.concatenate(is_, axis=-1) + bi_ref[...])
    nl = -lam_ref[...]
    softplus = jnp.maximum(nl, 0.0) + jnp.log1p(jnp.exp(-jnp.abs(nl)))
    a = jnp.exp(-LRU_C * r * softplus)
    y = 1.0 - a * a
    bb = jnp.where(y > 0.0, y * lax.rsqrt(y), 0.0) * i * xc

    row = lax.broadcasted_iota(jnp.int32, (tile, REC_W), 0)
    d = 1
    while d < tile:
        keep = row >= d
        a_sh = jnp.where(keep, pltpu.roll(a, d, 0), 1.0)
        b_sh = jnp.where(keep, pltpu.roll(bb, d, 0), 0.0)
        bb = a * b_sh + bb
        a = a * a_sh
        d *= 2
    hh = bb + a * h_ref[0:1, :]
    h_ref[...] = jnp.broadcast_to(hh[tile - 1:tile, :], h_ref.shape)
    o_ref[...] = _gelu(yg_ref[...]) * hh


def _rglru(xr, yg, cw, cb, wr, br, wi, bi, lam, b, s, tile):
    nt = s // tile
    tok = pl.BlockSpec((tile, REC_W), lambda bi_, ti: (bi_ * nt + ti, 0))
    full = lambda a: pl.BlockSpec(a.shape, lambda bi_, ti: (0,) * a.ndim)
    return pl.pallas_call(
        _rglru_kernel,
        grid=(b, nt),
        in_specs=[tok, tok, full(cw), full(cb), full(wr), full(br), full(wi), full(bi), full(lam)],
        out_specs=tok,
        out_shape=jax.ShapeDtypeStruct((b * s, REC_W), _f32),
        scratch_shapes=[pltpu.VMEM((tile + 8, REC_W), _f32), pltpu.VMEM((8, REC_W), _f32)],
        compiler_params=_cparams(("arbitrary", "arbitrary")),
        name="rglru",
    )(xr, yg, cw, cb, wr, br, wi, bi, lam)


def _out_proj_kernel(x_ref, at_ref, rc_ref, ag_ref, rg_ref, wo_ref, fg_ref, x1_ref, xn_ref):
    an = _rms(at_ref[...], ag_ref[...]).astype(_bf16)
    rn = _rms(rc_ref[...], rg_ref[...]).astype(_bf16)
    y = jnp.dot(an, wo_ref[0:ATTN_W, :], preferred_element_type=_f32)
    y = y + jnp.dot(rn, wo_ref[ATTN_W:, :], preferred_element_type=_f32)
    x1 = x_ref[...] + y
    x1_ref[...] = x1
    xn_ref[...] = _rms(x1, fg_ref[...]).astype(_bf16)


def _out_proj(x2, attn, rec, ag, rg, wo, fg, tile):
    n, d = x2.shape
    tok = lambda w: pl.BlockSpec((tile, w), lambda i: (i, 0))
    full = lambda a: pl.BlockSpec(a.shape, lambda i: (0,) * a.ndim)
    return pl.pallas_call(
        _out_proj_kernel,
        grid=(n // tile,),
        in_specs=[tok(d), tok(ATTN_W), tok(REC_W), full(ag), full(rg), full(wo), full(fg)],
        out_specs=[tok(d), tok(d)],
        out_shape=[jax.ShapeDtypeStruct((n, d), _f32), jax.ShapeDtypeStruct((n, d), _bf16)],
        compiler_params=_cparams(("arbitrary",)),
        name="out_proj",
    )(x2, attn, rec, ag, rg, wo, fg)


def _extract_top(s, ids, big, payload=None):
    vals, sel, pay = [], [], []
    for _ in range(TOPK):
        m = jnp.max(s, axis=0, keepdims=True)
        i = jnp.min(jnp.where(s == m, ids, big), axis=0, keepdims=True)
        hit = ids == i
        vals.append(m)
        sel.append(i)
        if payload is not None:
            pay.append(jnp.max(jnp.where(hit, payload, -1), axis=0, keepdims=True))
        s = jnp.where(hit, NEG, s)
    return vals, sel, pay


CODE_ROWS = 16


def _extract_top_unique(scores, code, side_work=None):
    state = list(scores)
    res = [([], []) for _ in state]
    for rnd in range(TOPK):
        for k, s in enumerate(state):
            m = jnp.max(s, axis=0, keepdims=True)
            hit = s == m
            res[k][0].append(m)
            res[k][1].append(jnp.dot(code, jnp.where(hit, 1.0, 0.0).astype(_bf16), preferred_element_type=_f32))
            state[k] = jnp.where(hit, NEG, s)
        if side_work is not None:
            side_work(rnd)
    return res


W_PITCH = N_KEYS + 8
W_REGROUP = 16


def _peer_route_kernel(xn_ref, wq_ref, k1_ref, k2_ref, w_ref, ep_ref, gp_ref, wscr_ref, qs_ref, v1_ref, i1_ref,
                       v2_ref, i2_ref, tv_ref, pc_ref, qc_ref, et_ref, gt_ref):
    tile = xn_ref.shape[0]
    n_groups = N_HEADS // TOPK_CHAINS
    assert tile == TOPK_COLS and n_groups * 2 * TOPK * 2 == tile

    @pl.when(pl.program_id(0) == 0)
    def _():
        ep_ref[...] = jnp.zeros(ep_ref.shape, jnp.int32)
        gp_ref[...] = jnp.zeros(gp_ref.shape, _f32)

    sub = lax.broadcasted_iota(jnp.int32, (N_KEYS, N_SLOTS), 0)
    zero_blk = jnp.zeros((N_KEYS, N_SLOTS), _bf16)

    def build_pair(p):
        ga, hb = [], []
        for u in (2 * p, 2 * p + 1):
            e = ep_ref[u:u + 1, :]
            ga.append(jnp.where(sub == (e >> 7), gp_ref[u:u + 1, :], 0.0).astype(_bf16))
            hb.append(jnp.where(sub == (e & (N_KEYS - 1)), 1.0, 0.0).astype(_bf16))
        lhs = jnp.concatenate(ga, axis=1)
        rhs = jnp.concatenate([jnp.concatenate([hb[0], zero_blk], axis=1),
                               jnp.concatenate([zero_blk, hb[1]], axis=1)], axis=0)
        w2 = lax.dot_general(lhs, rhs, (((1,), (1,)), ((), ())), preferred_element_type=_f32)
        for k, u in enumerate((2 * p, 2 * p + 1)):
            wscr_ref[u * W_PITCH:u * W_PITCH + N_KEYS, :] = w2[:, k * N_KEYS:(k + 1) * N_KEYS]

    def group_queries(g):
        q = jnp.dot(xn_ref[...], wq_ref[g], preferred_element_type=_f32)
        return [q[:, c * 2 * PEER_HALF:(c + 1) * 2 * PEER_HALF].astype(_bf16) for c in range(TOPK_CHAINS)]

    def slot_rows(h):
        return pl.ds(h * TOPK if isinstance(h, int) else pl.multiple_of(h * TOPK, TOPK), TOPK)

    def project(g, slot):
        for c, qh in enumerate(group_queries(g)):
            qs_ref[slot, c] = qh
    key_id = lax.broadcasted_iota(jnp.int32, (N_KEYS, TOPK_COLS), 0)

    def cand_codes(row):
        low = row < TOPK
        mid = row < TOPK + 8 * 7
        cq = jnp.where(low, 0, jnp.where(mid, ((row - TOPK) >> 3) + 1, row - (TOPK + 8 * 7) + 8))
        cp = jnp.where(low, row, jnp.where(mid, (row - TOPK) & 7, 0))
        ok = ((cp + 1) * (cq + 1) <= TOPK) & (row < TOPK + 8 * 8)
        return cp, cq, ok

    n_rows = N_KEYS
    cp, cq, cand_ok = cand_codes(lax.broadcasted_iota(jnp.int32, (n_rows, TOPK_COLS), 0))
    cand_id = cp * TOPK + cq
    code_row = lax.broadcasted_iota(jnp.int32, (CODE_ROWS, n_rows), 0)
    code_col = lax.broadcasted_iota(jnp.int32, (CODE_ROWS, n_rows), 1)
    key_code = jnp.where(code_row == 0, code_col, jnp.where(code_row == 1, 1, 0)).astype(_f32).astype(_bf16)
    lp, lq, _ = cand_codes(code_col)
    pair_code = jnp.where(code_row == 0, lp, jnp.where(code_row == 1, lq, jnp.where(code_row == 2, 1, 0)))
    pair_code = pair_code.astype(_f32).astype(_bf16)

    def half_scores(h, qh, cols, half):
        kref = (k1_ref, k2_ref)[half]
        return lax.dot_general(kref[h], qh[cols, half * PEER_HALF:(half + 1) * PEER_HALF],
                               (((1,), (1,)), ((), ())), preferred_element_type=_f32)

    def candidates(c, cols):
        v1 = v1_ref[c, :, cols]
        i1 = i1_ref[c, :, cols] * N_KEYS
        cv = [v1 + v2_ref[c, 0:1, cols]]
        ce = [i1 + i2_ref[c, 0:1, cols]]
        for qq in range(1, 8):
            cv.append(v1[0:8] + v2_ref[c, qq:qq + 1, cols])
            ce.append(i1[0:8] + i2_ref[c, qq:qq + 1, cols])
        cv.append(v1[0:1] + v2_ref[c, 8:16, cols])
        ce.append(i1[0:1] + i2_ref[c, 8:16, cols])
        pad = n_rows - (TOPK + 8 * 8)
        cv.append(jnp.full((pad, TOPK_COLS), NEG, _f32))
        ce.append(jnp.zeros((pad, TOPK_COLS), jnp.int32))
        return jnp.where(cand_ok, jnp.concatenate(cv, axis=0), NEG), jnp.concatenate(ce, axis=0)

    def gates(c, h):
        tv = tv_ref[c]
        ex = jnp.exp(tv - tv[0:1, :])
        gt_ref[slot_rows(h), :] = ex / jnp.sum(ex, axis=0, keepdims=True)

    def head_group(i, slot, most_hits):
        heads = [(c, i * TOPK_CHAINS + c) for c in range(TOPK_CHAINS)]
        for lt in range(tile // TOPK_COLS):
            cols = slice(lt * TOPK_COLS, (lt + 1) * TOPK_COLS)
            halves = [(c, h, half) for c, h in heads for half in range(2)]
            found = _extract_top_unique([half_scores(h, qs_ref[slot, c], cols, half) for c, h, half in halves],
                                        key_code, lambda rnd: build_pair(i * 2 * TOPK + rnd))
            for (c, h, half), (vals, outs) in zip(halves, found):
                vref, iref = ((v1_ref, i1_ref), (v2_ref, i2_ref))[half]
                for kk in range(TOPK):
                    vref[c, kk:kk + 1, cols] = vals[kk]
                    iref[c, kk:kk + 1, cols] = outs[kk][0:1].astype(jnp.int32)
                    most_hits = jnp.maximum(most_hits, outs[kk][1:2])
            found = _extract_top_unique([candidates(c, cols)[0] for c, _ in heads], pair_code,
                                        lambda rnd: build_pair(i * 2 * TOPK + TOPK + rnd))
            for (c, h), (top, outs) in zip(heads, found):
                for kk in range(TOPK):
                    tv_ref[c, kk:kk + 1, cols] = top[kk]
                    pc_ref[c, kk:kk + 1, cols] = outs[kk][0:1]
                    qc_ref[c, kk:kk + 1, cols] = outs[kk][1:2]
                    most_hits = jnp.maximum(most_hits, outs[kk][2:3])
                pc = pc_ref[c, :, cols]
                qc = qc_ref[c, :, cols]
                e1 = jnp.zeros((TOPK, TOPK_COLS), jnp.int32)
                e2 = jnp.zeros((TOPK, TOPK_COLS), jnp.int32)
                for r in range(TOPK):
                    e1 = jnp.where(pc == float(r), i1_ref[c, r:r + 1, cols], e1)
                    e2 = jnp.where(qc == float(r), i2_ref[c, r:r + 1, cols], e2)
                et_ref[lt, slot_rows(h), :] = e1 * N_KEYS + e2
        for c, h in heads:
            gates(c, h)
        return most_hits

    def group_general(g, carry):
        for c, qh in enumerate(group_queries(g)):
            h = g * TOPK_CHAINS + c
            for lt in range(tile // TOPK_COLS):
                cols = slice(lt * TOPK_COLS, (lt + 1) * TOPK_COLS)
                for half, (vref, iref) in enumerate(((v1_ref, i1_ref), (v2_ref, i2_ref))):
                    vals, sel, _ = _extract_top(half_scores(h, qh, cols, half), key_id, N_KEYS)
                    for kk in range(TOPK):
                        vref[c, kk:kk + 1, cols] = vals[kk]
                        iref[c, kk:kk + 1, cols] = sel[kk]
                cv, ce = candidates(c, cols)
                top, _, picked = _extract_top(cv, cand_id, TOPK * TOPK, payload=ce)
                for kk in range(TOPK):
                    tv_ref[c, kk:kk + 1, cols] = top[kk]
                    et_ref[lt, pl.ds(h * TOPK + kk, 1), :] = picked[kk]
            gates(c, h)
        return carry

    most_hits = jnp.zeros((1, TOPK_COLS), _f32)
    project(0, 0)
    for g in range(n_groups):
        if g + 1 < n_groups:
            project(g + 1, (g + 1) % 2)
        most_hits = head_group(g, g % 2, most_hits)

    @pl.when(jnp.max(most_hits) > 1.5)
    def _():
        lax.fori_loop(0, n_groups, group_general, 0)

    def regroup(j, c):
        for t in range(W_REGROUP):
            r = j * W_REGROUP + t
            w_ref[r] = wscr_ref[pl.ds(r, tile, stride=W_PITCH), :].astype(_bf16)
        return c

    lax.fori_loop(0, N_KEYS // W_REGROUP, regroup, 0)
    ep_ref[...] = et_ref[0].T
    gp_ref[...] = gt_ref[...].T


def _peer_route(xn, wq_g, k1, k2, tile):
    n, d = xn.shape
    nt = n // tile
    const = lambda a: pl.BlockSpec(a.shape, lambda i: (0,) * a.ndim, pipeline_mode=pl.Buffered(1))
    return pl.pallas_call(
        _peer_route_kernel,
        grid=(nt + 1,),
        in_specs=[pl.BlockSpec((tile, d), lambda i: (jnp.minimum(i, nt - 1), 0)), const(wq_g), const(k1), const(k2)],
        out_specs=pl.BlockSpec((N_KEYS, tile, N_KEYS), lambda i: (0, jnp.maximum(i - 1, 0), 0)),
        out_shape=jax.ShapeDtypeStruct((N_KEYS, n, N_KEYS), _bf16),
        scratch_shapes=[pltpu.VMEM((tile, N_SLOTS), jnp.int32), pltpu.VMEM((tile, N_SLOTS), _f32),
                        pltpu.VMEM((tile * W_PITCH, N_KEYS), _f32),
                        pltpu.VMEM((2, TOPK_CHAINS, tile, 2 * PEER_HALF), _bf16),
                        pltpu.VMEM((TOPK_CHAINS, TOPK, tile), _f32), pltpu.VMEM((TOPK_CHAINS, TOPK, tile), jnp.int32),
                        pltpu.VMEM((TOPK_CHAINS, TOPK, tile), _f32), pltpu.VMEM((TOPK_CHAINS, TOPK, tile), jnp.int32),
                        pltpu.VMEM((TOPK_CHAINS, TOPK, tile), _f32), pltpu.VMEM((TOPK_CHAINS, TOPK, tile), _f32),
                        pltpu.VMEM((TOPK_CHAINS, TOPK, tile), _f32),
                        pltpu.VMEM((tile // TOPK_COLS, N_SLOTS, TOPK_COLS), jnp.int32),
                        pltpu.VMEM((N_SLOTS, tile), _f32)],
        compiler_params=_cparams(("arbitrary",)),
        name="peer_route",
    )(xn, wq_g, k1, k2)


def _peer_ffn_kernel(nblk, xn_ref, x1_ref, ut_ref, v_ref, w_ref, o_ref, pa_ref, pb_ref):
    j = pl.program_id(1)
    rows = w_ref.shape[0]

    @pl.when(j == 0)
    def _():
        o_ref[...] = x1_ref[...]
        pb_ref[...] = jnp.zeros(pb_ref.shape, _bf16)

    def step(p_new_ref, p_prev_ref):
        if p_new_ref is not None:
            a = lax.dot_general(xn_ref[...], ut_ref[...], (((1,), (1,)), ((), ())), preferred_element_type=_f32)
            w = jnp.concatenate([w_ref[r] for r in range(rows)], axis=-1).astype(_f32)
            p_new_ref[...] = (_gelu(a) * w).astype(_bf16)
        o_ref[...] += jnp.dot(p_prev_ref[...], v_ref[...], preferred_element_type=_f32)

    last = nblk
    assert nblk % 2 == 0

    @pl.when((lax.rem(j, 2) == 0) & (j < last))
    def _():
        step(pa_ref, pb_ref)

    @pl.when(lax.rem(j, 2) == 1)
    def _():
        step(pb_ref, pa_ref)

    @pl.when(j == last)
    def _():
        step(None, pb_ref)


def _peer_ffn(xn, x1, ut, v, wb, tile, eblk):
    n, d = xn.shape
    nblk = v.shape[0] // eblk
    rows = eblk // N_KEYS
    cur = lambda j: jnp.minimum(j, nblk - 1)
    prev = lambda j: jnp.maximum(j - 1, 0)
    return pl.pallas_call(
        functools.partial(_peer_ffn_kernel, nblk),
        grid=(n // tile, nblk + 1),
        in_specs=[
            pl.BlockSpec((tile, d), lambda i, j: (i, 0), pipeline_mode=pl.Buffered(1)),
            pl.BlockSpec((tile, d), lambda i, j: (i, 0), pipeline_mode=pl.Buffered(1)),
            pl.BlockSpec((eblk, d), lambda i, j: (cur(j), 0)),
            pl.BlockSpec((eblk, d), lambda i, j: (prev(j), 0)),
            pl.BlockSpec((rows, tile, N_KEYS), lambda i, j: (cur(j), i, 0)),
        ],
        out_specs=pl.BlockSpec((tile, d), lambda i, j: (i, 0)),
        out_shape=jax.ShapeDtypeStruct((n, d), _f32),
        scratch_shapes=[pltpu.VMEM((tile, eblk), _bf16), pltpu.VMEM((tile, eblk), _bf16)],
        compiler_params=_cparams(("arbitrary", "arbitrary")),
        name="peer_ffn",
    )(xn, x1, ut, v, wb)


def _pad_rope_cols(w):
    lead = w.shape[:-1]
    w = w.reshape(*lead, N_HEADS, QK_DIM)
    w = jnp.pad(w, [(0, 0)] * len(lead) + [(0, 0), (0, QK_PAD - QK_DIM)])
    return w.reshape(*lead, N_HEADS * QK_PAD)


def _layer(x2, pos, invf, b, s, tiles, mix_g, w_in, q_g, w_uq, kv_g, w_ukv, qh_g, kh_g, conv_w, conv_b, w_rg,
           b_rg, w_ig, b_ig, lam, ao_g, ro_g, w_out, ffn_g, w_q, k1, k2, u_tab, v_tab):
    d = x2.shape[1]
    row = lambda a: a.reshape(1, -1)
    off_kr = Q_RANK + KV_RANK
    off_xr = off_kr + ROPE
    w_in_p = jnp.concatenate(
        [w_in[:, :off_xr], jnp.zeros((d, LANES - ROPE), w_in.dtype), w_in[:, off_xr:]], axis=1).astype(_bf16)
    cq, ckv, kr, xr, yg = _proj(x2, row(mix_g), w_in_p, tiles["proj"])

    pad_g = lambda g: jnp.pad(g, (0, QK_PAD - QK_DIM)).reshape(1, QK_PAD)
    q, k, v = _mla_prep(cq, ckv, kr, pos, invf, row(q_g), _pad_rope_cols(w_uq).astype(_bf16), row(kv_g),
                        w_ukv.astype(_bf16), pad_g(qh_g), pad_g(kh_g), b, s, tiles["prep"])
    attn = _attention(q, k, v, tiles["attn"]).reshape(b * s, ATTN_W)

    rec = _rglru(xr, yg, conv_w, row(conv_b), w_rg.astype(_bf16), row(b_rg), w_ig.astype(_bf16), row(b_ig),
                 row(lam), b, s, tiles["rec"])

    x1, xn = _out_proj(x2, attn, rec, row(ao_g), row(ro_g), w_out.astype(_bf16), row(ffn_g), tiles["out"])

    wq_g = w_q.reshape(d, N_HEADS // TOPK_CHAINS, -1).transpose(1, 0, 2).astype(_bf16)
    wb = _peer_route(xn, wq_g, k1.astype(_bf16), k2.astype(_bf16), tiles["topk"])
    return _peer_ffn(xn, x1, u_tab.astype(_bf16), v_tab.astype(_bf16), wb, tiles["ffn"], tiles["eblk"])


_TILES = dict(proj=512, prep=256, attn=2048, rec=256, out=512, topk=256, ffn=1024, eblk=512)


def _forward(tiles, x, positions, mix_norm_g, w_in, q_norm_g, w_uq, kv_norm_g, w_ukv, q_head_norm_g,
             k_head_norm_g, conv_w, conv_b, w_rgate, b_rgate, w_igate, b_igate, lru_lambda, attn_out_norm_g,
             rec_out_norm_g, w_out, ffn_norm_g, peer_w_q, peer_keys_1, peer_keys_2, peer_u, peer_v):
    b, s, d = x.shape
    half = ROPE // 2
    freq = ROPE_THETA ** (-jnp.arange(half, dtype=_f32) / half)
    invf = jnp.concatenate([freq, freq, jnp.zeros((LANES - ROPE,), _f32)]).reshape(1, LANES)
    pos = positions.reshape(b * s, 1)
    x2 = x.reshape(b * s, d)
    for l in range(mix_norm_g.shape[0]):
        x2 = _layer(x2, pos, invf, b, s, tiles, mix_norm_g[l], w_in[l], q_norm_g[l], w_uq[l], kv_norm_g[l],
                    w_ukv[l], q_head_norm_g[l], k_head_norm_g[l], conv_w[l], conv_b[l], w_rgate[l], b_rgate[l],
                    w_igate[l], b_igate[l], lru_lambda[l], attn_out_norm_g[l], rec_out_norm_g[l], w_out[l],
                    ffn_norm_g[l], peer_w_q[l], peer_keys_1[l], peer_keys_2[l], peer_u[l], peer_v[l])
    return x2.reshape(b, s, d)


def kernel(x, positions, mix_norm_g, w_in, q_norm_g, w_uq, kv_norm_g, w_ukv, q_head_norm_g, k_head_norm_g, conv_w, conv_b, w_rgate, b_rgate, w_igate, b_igate, lru_lambda, attn_out_norm_g, rec_out_norm_g, w_out, ffn_norm_g, peer_w_q, peer_keys_1, peer_keys_2, peer_u, peer_v):
    return _forward(_TILES, x, positions, mix_norm_g, w_in, q_norm_g, w_uq, kv_norm_g, w_ukv, q_head_norm_g,
                    k_head_norm_g, conv_w, conv_b, w_rgate, b_rgate, w_igate, b_igate, lru_lambda,
                    attn_out_norm_g, rec_out_norm_g, w_out, ffn_norm_g, peer_w_q, peer_keys_1, peer_keys_2,
                    peer_u, peer_v)
```
